```python
import math
import jax, jax.numpy as jnp
from jax import lax
import numpy as np


D_MODEL = 2048
BATCH = 4
SEQ = 2048
DEPTH = 4
DEC_BATCH = 16
DEC_SEQ = 2048
PAST_LEN = 128

MIX_WIDTH = D_MODEL
ATTN_WIDTH = MIX_WIDTH // 2
FOURIER_WIDTH = MIX_WIDTH - ATTN_WIDTH
N_HEADS = 8
HEAD_DIM = ATTN_WIDTH // (2 * N_HEADS)
V_DIM = 2 * HEAD_DIM
N_FOURIER_GROUPS = 4
FOURIER_GROUP = FOURIER_WIDTH // N_FOURIER_GROUPS
IN_WIDTH = 3 * ATTN_WIDTH + FOURIER_WIDTH
D_FF = 256 * ((8 * D_MODEL // 3 + 255) // 256)
N_EXPERTS = 8
TOP_K = 2
D_FF_EXPERT = 7 * D_MODEL // 2
N_DENSE = (DEPTH + 1) // 2
N_MOE = DEPTH // 2
ROPE_THETA = 10000.0
EPS = 1e-6
Q_BLOCK = 128

kernel_name = "hymba_diffattn_fnet_moe_encoder"


def rms_norm(x, g):
    xf = x.astype(jnp.float32)
    y = xf * lax.rsqrt(jnp.mean(xf * xf, axis=-1, keepdims=True) + EPS)
    return (y * g.astype(jnp.float32)).astype(x.dtype)


def rope_tables(s):
    pos = jnp.arange(s, dtype=jnp.float32)
    inv = ROPE_THETA ** (-jnp.arange(0, HEAD_DIM, 2, dtype=jnp.float32) / HEAD_DIM)
    ang = pos[:, None] * inv[None, :]
    return jnp.cos(ang), jnp.sin(ang)


def apply_rope(x, cos, sin):
    half = HEAD_DIM // 2
    xf = x.astype(jnp.float32)
    x1, x2 = xf[..., :half], xf[..., half:]
    c = cos[:, None, None, :]
    s = sin[:, None, None, :]
    return jnp.concatenate([x1 * c - x2 * s, x2 * c + x1 * s], axis=-1).astype(x.dtype)


def diff_attention(q, k, v, lam):
    b, s = q.shape[0], q.shape[1]
    n_blk = s // Q_BLOCK
    scale = HEAD_DIM ** -0.5
    q_blocks = jnp.moveaxis(q.reshape(b, n_blk, Q_BLOCK, N_HEADS, 2, HEAD_DIM), 1, 0)

    def block(qi):
        sc = jnp.einsum("bqhcd,bkhcd->bhcqk", qi, k, preferred_element_type=jnp.float32) * scale
        p = jax.nn.softmax(sc, axis=-1)
        w = (p[:, :, 0] - lam * p[:, :, 1]).astype(v.dtype)
        return jnp.einsum("bhqk,bkhe->bqhe", w, v)

    o = lax.map(block, q_blocks)
    return jnp.moveaxis(o, 0, 1).reshape(b, s, N_HEADS, V_DIM)


def mixer(xn, w_in, w_out, q_g, k_g, lq1, lk1, lq2, lk2, subln_g, fourier_g, layer, cos, sin):
    b, s, _ = xn.shape
    proj = xn @ w_in
    q, k, v, u = jnp.split(proj, [ATTN_WIDTH, 2 * ATTN_WIDTH, 3 * ATTN_WIDTH], axis=-1)
    q = apply_rope(rms_norm(q.reshape(b, s, N_HEADS, 2, HEAD_DIM), q_g), cos, sin)
    k = apply_rope(rms_norm(k.reshape(b, s, N_HEADS, 2, HEAD_DIM), k_g), cos, sin)
    v = v.reshape(b, s, N_HEADS, V_DIM)
    lam_init = 0.8 - 0.6 * math.exp(-0.3 * layer)
    lam = (jnp.exp(jnp.sum(lq1.astype(jnp.float32) * lk1.astype(jnp.float32)))
           - jnp.exp(jnp.sum(lq2.astype(jnp.float32) * lk2.astype(jnp.float32))) + lam_init)
    o = diff_attention(q, k, v, lam)
    o = (rms_norm(o, subln_g) * (1.0 - lam_init)).reshape(b, s, ATTN_WIDTH)
    uf = u.astype(jnp.float32).reshape(b, s, N_FOURIER_GROUPS, FOURIER_GROUP)
    f = jnp.fft.fft2(uf, axes=(1, 3), norm="ortho").real
    f = rms_norm(f, fourier_g.reshape(N_FOURIER_GROUPS, FOURIER_GROUP))
    f = f.astype(xn.dtype).reshape(b, s, FOURIER_WIDTH)
    return jnp.concatenate([o, f], axis=-1) @ w_out


def swiglu(x, wg, wu, wd):
    return (jax.nn.silu(x @ wg) * (x @ wu)) @ wd


def moe_swiglu(x, w_router, wg, wu, wd):
    b, s, d = x.shape
    t = x.reshape(b * s, d)
    logits = (t @ w_router).astype(jnp.float32)
    top_v, top_i = lax.top_k(logits, TOP_K)
    gates = jax.nn.softmax(top_v, axis=-1)
    combine = jnp.sum(jax.nn.one_hot(top_i, N_EXPERTS, dtype=jnp.float32) * gates[..., None], axis=1)
    y = jnp.zeros((b * s, d), jnp.float32)
    for e in range(N_EXPERTS):
        y = y + combine[:, e:e + 1] * swiglu(t, wg[e], wu[e], wd[e]).astype(jnp.float32)
    return y.astype(x.dtype).reshape(b, s, d)


def trunk(x, attn_norm_g, w_in, q_norm_g, k_norm_g, lambda_q1, lambda_k1, lambda_q2, lambda_k2,
          subln_g, fourier_norm_g, w_out, ffn_norm_g, dense_w_gate, dense_w_up, dense_w_down,
          router_w, moe_w_gate, moe_w_up, moe_w_down):
    cos, sin = rope_tables(x.shape[1])
    for l in range(DEPTH):
        xn = rms_norm(x, attn_norm_g[l])
        x = x + mixer(xn, w_in[l], w_out[l], q_norm_g[l], k_norm_g[l], lambda_q1[l], lambda_k1[l],
                      lambda_q2[l], lambda_k2[l], subln_g[l], fourier_norm_g[l], l, cos, sin)
        xn = rms_norm(x, ffn_norm_g[l])
        if l % 2 == 0:
            i = l // 2
            x = x + swiglu(xn, dense_w_gate[i], dense_w_up[i], dense_w_down[i])
        else:
            i = l // 2
            x = x + moe_swiglu(xn, router_w[i], moe_w_gate[i], moe_w_up[i], moe_w_down[i])
    return x


def setup_inputs(seed: int = 0) -> dict:
    key = jax.random.key(seed)
    ks = jax.random.split(key, 21)
    f32 = jnp.float32
    nrm = lambda k, shape, scale: jax.random.normal(k, shape, f32) * scale
    gain = lambda k, shape: 1.0 + 0.02 * jax.random.normal(k, shape, f32)
    return {
        "x_prompt": nrm(ks[0], (BATCH, SEQ, D_MODEL), 1.0),
        "x_sample": nrm(ks[1], (DEC_BATCH, DEC_SEQ, D_MODEL), 1.0),
        "attn_norm_g": gain(ks[2], (DEPTH, D_MODEL)),
        "w_in": nrm(ks[3], (DEPTH, D_MODEL, IN_WIDTH), D_MODEL ** -0.5),
        "q_norm_g": gain(ks[4], (DEPTH, HEAD_DIM)),
        "k_norm_g": gain(ks[5], (DEPTH, HEAD_DIM)),
        "lambda_q1": nrm(ks[6], (DEPTH, HEAD_DIM), 0.1),
        "lambda_k1": nrm(ks[7], (DEPTH, HEAD_DIM), 0.1),
        "lambda_q2": nrm(ks[8], (DEPTH, HEAD_DIM), 0.1),
        "lambda_k2": nrm(ks[9], (DEPTH, HEAD_DIM), 0.1),
        "subln_g": gain(ks[10], (DEPTH, V_DIM)),
        "fourier_norm_g": gain(ks[11], (DEPTH, FOURIER_WIDTH)),
        "w_out": nrm(ks[12], (DEPTH, MIX_WIDTH, D_MODEL), MIX_WIDTH ** -0.5),
        "ffn_norm_g": gain(ks[13], (DEPTH, D_MODEL)),
        "dense_w_gate": nrm(ks[14], (N_DENSE, D_MODEL, D_FF), D_MODEL ** -0.5),
        "dense_w_up": nrm(ks[15], (N_DENSE, D_MODEL, D_FF), D_MODEL ** -0.5),
        "dense_w_down": nrm(ks[16], (N_DENSE, D_FF, D_MODEL), D_FF ** -0.5),
        "router_w": nrm(ks[17], (N_MOE, D_MODEL, N_EXPERTS), D_MODEL ** -0.5),
        "moe_w_gate": nrm(ks[18], (N_MOE, N_EXPERTS, D_MODEL, D_FF_EXPERT), D_MODEL ** -0.5),
        "moe_w_up": nrm(ks[19], (N_MOE, N_EXPERTS, D_MODEL, D_FF_EXPERT), D_MODEL ** -0.5),
        "moe_w_down": nrm(ks[20], (N_MOE, N_EXPERTS, D_FF_EXPERT, D_MODEL), D_FF_EXPERT ** -0.5),
    }


def reference(x_prompt, x_sample, attn_norm_g, w_in, q_norm_g, k_norm_g, lambda_q1, lambda_k1,
              lambda_q2, lambda_k2, subln_g, fourier_norm_g, w_out, ffn_norm_g, dense_w_gate,
              dense_w_up, dense_w_down, router_w, moe_w_gate, moe_w_up, moe_w_down):
    y_prompt = trunk(x_prompt, attn_norm_g, w_in, q_norm_g, k_norm_g, lambda_q1, lambda_k1,
                     lambda_q2, lambda_k2, subln_g, fourier_norm_g, w_out, ffn_norm_g,
                     dense_w_gate, dense_w_up, dense_w_down, router_w, moe_w_gate, moe_w_up,
                     moe_w_down)
    y_sample = trunk(x_sample, attn_norm_g, w_in, q_norm_g, k_norm_g, lambda_q1, lambda_k1,
                     lambda_q2, lambda_k2, subln_g, fourier_norm_g, w_out, ffn_norm_g,
                     dense_w_gate, dense_w_up, dense_w_down, router_w, moe_w_gate, moe_w_up,
                     moe_w_down)
    return (y_prompt, y_sample)
```

```python
import functools
import math

import jax
import jax.numpy as jnp
from jax import lax
from jax.experimental import pallas as pl
from jax.experimental.pallas import tpu as pltpu

D_MODEL = 2048
ATTN_WIDTH = 1024
FOURIER_WIDTH = 1024
N_HEADS = 8
HEAD_DIM = 64
V_DIM = 128
N_FOURIER_GROUPS = 4
FOURIER_GROUP = 256
IN_WIDTH = 4096
N_EXPERTS = 8
TOP_K = 2
ROPE_THETA = 10000.0
EPS = 1e-6
SCORE_SCALE = HEAD_DIM ** -0.5

LANES = 128
VMEM_LIMIT = 56 * 1024 * 1024

F32 = jnp.float32
BF16 = jnp.bfloat16


def _params(sem, vmem=VMEM_LIMIT):
    return pltpu.CompilerParams(dimension_semantics=sem, vmem_limit_bytes=vmem)


def _in_proj_kernel(x_ref, g_ref, w_ref, o_ref, xn_ref):
    @pl.when(pl.program_id(1) == 0)
    def _():
        x = x_ref[...]
        ms = jnp.mean(x * x, axis=-1, keepdims=True)
        xn_ref[...] = (x * lax.rsqrt(ms + EPS) * g_ref[...]).astype(BF16)

    o_ref[...] = jnp.dot(xn_ref[...], w_ref[...], preferred_element_type=F32).astype(o_ref.dtype)


def _in_proj(x, g, w, tm, tn):
    t, d = x.shape
    n = w.shape[1]
    return pl.pallas_call(
        _in_proj_kernel,
        grid=(t // tm, n // tn),
        in_specs=[pl.BlockSpec((tm, d), lambda i, j: (i, 0)),
                  pl.BlockSpec((1, d), lambda i, j: (0, 0)),
                  pl.BlockSpec((d, tn), lambda i, j: (0, j))],
        out_specs=pl.BlockSpec((tm, tn), lambda i, j: (i, j)),
        out_shape=jax.ShapeDtypeStruct((t, n), BF16),
        scratch_shapes=[pltpu.VMEM((tm, d), BF16)],
        compiler_params=_params(("parallel", "arbitrary")),
        name="in_proj",
    )(x, g, w)


def _attn_kernel(lamp_ref, q_ref, k_ref, v_ref, cq_ref, sq_ref, ck_ref, sk_ref,
                 qg_ref, kg_ref, sg_ref, o_ref, kbuf, *, lam_init):
    lane = lax.broadcasted_iota(jnp.int32, (1, LANES), 1)
    lo = lane < HEAD_DIM
    first = (lane % HEAD_DIM) < (HEAD_DIM // 2)

    def norm_rope(x, g, c, s):
        x2 = x * x
        ss_lo = jnp.sum(jnp.where(lo, x2, 0.0), axis=-1, keepdims=True)
        ss_hi = jnp.sum(jnp.where(lo, 0.0, x2), axis=-1, keepdims=True)
        ms = jnp.where(lo, ss_lo, ss_hi) * (1.0 / HEAD_DIM)
        y = x * lax.rsqrt(ms + EPS) * g
        partner = jnp.where(first, pltpu.roll(y, LANES - HEAD_DIM // 2, 1),
                            pltpu.roll(y, HEAD_DIM // 2, 1))
        return y * c + partner * s

    @pl.when(pl.program_id(2) == 0)
    def _():
        kbuf[...] = norm_rope(k_ref[...].astype(F32), kg_ref[...], ck_ref[...], sk_ref[...]).astype(BF16)

    lp = lamp_ref[...]
    a1 = jnp.sum(lp[0:1] * lp[1:2], axis=-1, keepdims=True)
    a2 = jnp.sum(lp[2:3] * lp[3:4], axis=-1, keepdims=True)
    lam = jnp.exp(a1) - jnp.exp(a2) + lam_init

    q = norm_rope(q_ref[...].astype(F32), qg_ref[...], cq_ref[...], sq_ref[...]) * SCORE_SCALE
    q0 = jnp.where(lo, q, 0.0).astype(BF16)
    q1 = jnp.where(lo, 0.0, q).astype(BF16)
    k = kbuf[...]
    nt = (((1,), (1,)), ((), ()))
    s0 = lax.dot_general(q0, k, nt, preferred_element_type=F32)
    s1 = lax.dot_general(q1, k, nt, preferred_element_type=F32)
    e0 = jnp.exp(s0 - jnp.max(s0, axis=-1, keepdims=True))
    e1 = jnp.exp(s1 - jnp.max(s1, axis=-1, keepdims=True))
    r0 = 1.0 / jnp.sum(e0, axis=-1, keepdims=True)
    r1 = lam / jnp.sum(e1, axis=-1, keepdims=True)
    w = (e0 * r0 - e1 * r1).astype(BF16)
    o = jnp.dot(w, v_ref[...], preferred_element_type=F32)
    ms = jnp.mean(o * o, axis=-1, keepdims=True)
    o_ref[...] = (o * lax.rsqrt(ms + EPS) * sg_ref[...] * (1.0 - lam_init)).astype(o_ref.dtype)


def _attention(proj, lamp, cos_t, sin_t, qg, kg, sg, b, s, tq, lam_init):
    nq = s // tq
    row = lambda bi, h, qi: (bi * nq + qi, h)
    const = lambda bi, h, qi: (0, 0)
    return pl.pallas_call(
        functools.partial(_attn_kernel, lam_init=lam_init),
        grid=(b, N_HEADS, nq),
        in_specs=[pl.BlockSpec((4, HEAD_DIM), const),
                  pl.BlockSpec((tq, LANES), row),
                  pl.BlockSpec((s, LANES), lambda bi, h, qi: (bi, N_HEADS + h)),
                  pl.BlockSpec((s, LANES), lambda bi, h, qi: (bi, 2 * N_HEADS + h)),
                  pl.BlockSpec((tq, LANES), lambda bi, h, qi: (qi, 0)),
                  pl.BlockSpec((tq, LANES), lambda bi, h, qi: (qi, 0)),
                  pl.BlockSpec((s, LANES), const),
                  pl.BlockSpec((s, LANES), const),
                  pl.BlockSpec((1, LANES), const),
                  pl.BlockSpec((1, LANES), const),
                  pl.BlockSpec((1, LANES), const)],
        out_specs=pl.BlockSpec((tq, LANES), row),
        out_shape=jax.ShapeDtypeStruct((b * s, ATTN_WIDTH), BF16),
        scratch_shapes=[pltpu.VMEM((s, LANES), BF16)],
        compiler_params=_params(("parallel", "parallel", "arbitrary")),
        name="diff_attn",
    )(lamp, proj, proj, proj, cos_t, sin_t, cos_t, sin_t, qg, kg, sg)


def _fourier_kernel(u_ref, cs_ref, cc_ref, g_ref, o_ref, pq, *, s):
    fg = FOURIER_GROUP

    @pl.when(pl.program_id(1) == 0)
    def _():
        for gi in range(N_FOURIER_GROUPS):
            r = jnp.dot(u_ref[:, gi * fg:(gi + 1) * fg], cc_ref[...], preferred_element_type=F32)
            pq[0:s, gi * fg:(gi + 1) * fg] = r[:, :fg].astype(BF16)
            pq[s:2 * s, gi * fg:(gi + 1) * fg] = r[:, fg:].astype(BF16)

    f = jnp.dot(cs_ref[...], pq[...], preferred_element_type=F32)
    for gi in range(N_FOURIER_GROUPS):
        fgi = f[:, gi * fg:(gi + 1) * fg]
        ms = jnp.mean(fgi * fgi, axis=-1, keepdims=True)
        o_ref[:, gi * fg:(gi + 1) * fg] = (
            fgi * lax.rsqrt(ms + EPS) * g_ref[:, gi * fg:(gi + 1) * fg]).astype(o_ref.dtype)


def _fourier(proj, cs, cc, g, b, s, tm):
    n = s // tm
    return pl.pallas_call(
        functools.partial(_fourier_kernel, s=s),
        grid=(b, n),
        in_specs=[pl.BlockSpec((s, FOURIER_WIDTH), lambda bi, i: (bi, 3)),
                  pl.BlockSpec((tm, 2 * s), lambda bi, i: (i, 0)),
                  pl.BlockSpec((FOURIER_GROUP, 2 * FOURIER_GROUP), lambda bi, i: (0, 0)),
                  pl.BlockSpec((1, FOURIER_WIDTH), lambda bi, i: (0, 0))],
        out_specs=pl.BlockSpec((tm, FOURIER_WIDTH), lambda bi, i: (bi * n + i, 0)),
        out_shape=jax.ShapeDtypeStruct((b * s, FOURIER_WIDTH), BF16),
        scratch_shapes=[pltpu.VMEM((2 * s, FOURIER_WIDTH), BF16)],
        compiler_params=_params(("parallel", "arbitrary")),
        name="fourier_mix",
    )(proj, cs, cc, g)


def _out_proj_kernel(x_ref, a_ref, f_ref, w1_ref, w2_ref, o_ref):
    o_ref[...] = (x_ref[...]
                  + jnp.dot(a_ref[...], w1_ref[...], preferred_element_type=F32)
                  + jnp.dot(f_ref[...], w2_ref[...], preferred_element_type=F32))


def _out_proj(x, a, f, w, tm, tn):
    t, d = x.shape
    return pl.pallas_call(
        _out_proj_kernel,
        grid=(t // tm, d // tn),
        in_specs=[pl.BlockSpec((tm, tn), lambda i, j: (i, j)),
                  pl.BlockSpec((tm, ATTN_WIDTH), lambda i, j: (i, 0)),
                  pl.BlockSpec((tm, FOURIER_WIDTH), lambda i, j: (i, 0)),
                  pl.BlockSpec((ATTN_WIDTH, tn), lambda i, j: (0, j)),
                  pl.BlockSpec((FOURIER_WIDTH, tn), lambda i, j: (1, j))],
        out_specs=pl.BlockSpec((tm, tn), lambda i, j: (i, j)),
        out_shape=jax.ShapeDtypeStruct((t, d), F32),
        compiler_params=_params(("parallel", "arbitrary")),
        name="out_proj",
    )(x, a, f, w, w)


def _ffn_kernel(eid_ref, nv_ref, x_ref, g_ref, wg_ref, wu_ref, wd_ref, o_ref, xn_ref, *, residual):
    i = pl.program_id(0)
    f = pl.program_id(1)

    @pl.when(i < nv_ref[0])
    def _():
        @pl.when(f == 0)
        def _():
            x = x_ref[...]
            ms = jnp.mean(x * x, axis=-1, keepdims=True)
            xn_ref[...] = (x * lax.rsqrt(ms + EPS) * g_ref[...]).astype(BF16)
            o_ref[...] = x if residual else jnp.zeros_like(x)

        xn = xn_ref[...]
        gate = jnp.dot(xn, wg_ref[...], preferred_element_type=F32)
        up = jnp.dot(xn, wu_ref[...], preferred_element_type=F32)
        h = (gate * (1.0 / (1.0 + jnp.exp(-gate))) * up).astype(BF16)
        o_ref[...] += jnp.dot(h, wd_ref[...], preferred_element_type=F32)

    @pl.when((i >= nv_ref[0]) & (f == 0))
    def _():
        o_ref[...] = jnp.zeros_like(o_ref)


def _ffn(x, g, wg, wu, wd, eid, nvalid, tm, tf, residual):
    r, d = x.shape
    ff = wg.shape[2]
    nf = ff // tf
    last = nf - 1

    def row(i, f, eid_ref, nv_ref):
        return (jnp.minimum(i, nv_ref[0] - 1), 0)

    def fsel(i, f, nv_ref):
        return jnp.where(i < nv_ref[0], f, last)

    def up_map(i, f, eid_ref, nv_ref):
        return (eid_ref[jnp.minimum(i, nv_ref[0] - 1)], 0, fsel(i, f, nv_ref))

    def down_map(i, f, eid_ref, nv_ref):
        return (eid_ref[jnp.minimum(i, nv_ref[0] - 1)], fsel(i, f, nv_ref), 0)

    grid_spec = pltpu.PrefetchScalarGridSpec(
        num_scalar_prefetch=2,
        grid=(r // tm, nf),
        in_specs=[pl.BlockSpec((tm, d), row),
                  pl.BlockSpec((1, d), lambda i, f, e, n: (0, 0)),
                  pl.BlockSpec((None, d, tf), up_map),
                  pl.BlockSpec((None, d, tf), up_map),
                  pl.BlockSpec((None, tf, d), down_map)],
        out_specs=pl.BlockSpec((tm, d), lambda i, f, e, n: (i, 0)),
        scratch_shapes=[pltpu.VMEM((tm, d), BF16)],
    )
    return pl.pallas_call(
        functools.partial(_ffn_kernel, residual=residual),
        grid_spec=grid_spec,
        out_shape=jax.ShapeDtypeStruct((r, d), F32),
        compiler_params=_params(("arbitrary", "arbitrary")),
        name="swiglu_res" if residual else "swiglu_grouped",
    )(eid, nvalid, x, g, wg, wu, wd)


def _router_kernel(x_ref, g_ref, wh_ref, wl_ref, ri_ref, rf_ref, cnt_ref, carry, *, tm):
    @pl.when(pl.program_id(0) == 0)
    def _():
        carry[...] = jnp.zeros_like(carry)

    x = x_ref[...]
    ms = jnp.mean(x * x, axis=-1, keepdims=True)
    xn = x * lax.rsqrt(ms + EPS) * g_ref[...]
    xh = xn.astype(BF16)
    xl = (xn - xh.astype(F32)).astype(BF16)
    wh = wh_ref[...]
    logits = (jnp.dot(xh, wh, preferred_element_type=F32)
              + jnp.dot(xl, wh, preferred_element_type=F32)
              + jnp.dot(xh, wl_ref[...], preferred_element_type=F32))

    lane = lax.broadcasted_iota(jnp.int32, (tm, LANES), 1).astype(F32)
    neg = jnp.float32(-jnp.inf)
    lg = jnp.where(lane < N_EXPERTS, logits, neg)
    m1 = jnp.max(lg, axis=-1, keepdims=True)
    i1 = jnp.min(jnp.where(lg == m1, lane, float(LANES)), axis=-1, keepdims=True)
    lg2 = jnp.where(lane == i1, neg, lg)
    m2 = jnp.max(lg2, axis=-1, keepdims=True)
    i2 = jnp.min(jnp.where(lg2 == m2, lane, float(LANES)), axis=-1, keepdims=True)
    e = jnp.exp(m2 - m1)
    g1 = 1.0 / (1.0 + e)
    g2 = e / (1.0 + e)

    sel1 = lane == i1
    sel2 = lane == i2
    onehot = jnp.where(sel1 | sel2, 1.0, 0.0)
    r_i = lax.broadcasted_iota(jnp.int32, (tm, tm), 0)
    c_i = lax.broadcasted_iota(jnp.int32, (tm, tm), 1)
    tri = jnp.where(r_i > c_i, 1.0, 0.0).astype(BF16)
    before = jnp.dot(tri, onehot.astype(BF16), preferred_element_type=F32) + carry[...]
    rank1 = jnp.sum(jnp.where(sel1, before, 0.0), axis=-1, keepdims=True)
    rank2 = jnp.sum(jnp.where(sel2, before, 0.0), axis=-1, keepdims=True)
    carry[...] = carry[...] + jnp.sum(onehot, axis=0, keepdims=True)
    cnt_ref[...] = carry[...]

    ri = jnp.where(lane == 0.0, i1, jnp.where(lane == 1.0, i2,
         jnp.where(lane == 2.0, rank1, jnp.where(lane == 3.0, rank2, 0.0))))
    ri_ref[...] = ri.astype(jnp.int32)
    rf_ref[...] = jnp.where(lane == 0.0, g1, jnp.where(lane == 1.0, g2, 0.0))


def _router(x, g, wh, wl, tm):
    t, d = x.shape
    return pl.pallas_call(
        functools.partial(_router_kernel, tm=tm),
        grid=(t // tm,),
        in_specs=[pl.BlockSpec((tm, d), lambda i: (i, 0)),
                  pl.BlockSpec((1, d), lambda i: (0, 0)),
                  pl.BlockSpec((d, LANES), lambda i: (0, 0)),
                  pl.BlockSpec((d, LANES), lambda i: (0, 0))],
        out_specs=[pl.BlockSpec((tm, LANES), lambda i: (i, 0)),
                   pl.BlockSpec((tm, LANES), lambda i: (i, 0)),
                   pl.BlockSpec((1, LANES), lambda i: (0, 0))],
        out_shape=[jax.ShapeDtypeStruct((t, LANES), jnp.int32),
                   jax.ShapeDtypeStruct((t, LANES), F32),
                   jax.ShapeDtypeStruct((1, LANES), F32)],
        scratch_shapes=[pltpu.VMEM((1, LANES), F32)],
        compiler_params=_params(("arbitrary",)),
        name="router",
    )(x, g, wh, wl)


def _gather_kernel(src_ref, x_hbm, o_hbm, sem, *, chunk):
    i = pl.program_id(0)
    base = i * chunk

    def row_copy(src_row, dst_row):
        return pltpu.make_async_copy(x_hbm.at[pl.ds(src_row, 1)], o_hbm.at[pl.ds(dst_row, 1)], sem)

    def issue(j, c):
        row_copy(src_ref[j], base + j).start()
        return c

    lax.fori_loop(0, chunk, issue, 0)

    def drain(j, c):
        row_copy(0, 0).wait()
        return c

    lax.fori_loop(0, chunk, drain, 0)


def _gather_rows(x, src, chunk):
    r = src.shape[0]
    d = x.shape[1]
    return pl.pallas_call(
        functools.partial(_gather_kernel, chunk=chunk),
        grid=(r // chunk,),
        in_specs=[pl.BlockSpec((chunk,), lambda i: (i,), memory_space=pltpu.SMEM),
                  pl.BlockSpec(memory_space=pl.ANY)],
        out_specs=pl.BlockSpec(memory_space=pl.ANY),
        out_shape=jax.ShapeDtypeStruct((r, d), x.dtype),
        scratch_shapes=[pltpu.SemaphoreType.DMA(())],
        compiler_params=_params(("arbitrary",)),
        name="moe_dispatch",
    )(src, x)


def _combine_kernel(pos_ref, x_ref, rf_ref, y_hbm, o_ref, buf, sem, *, tc):
    def row_copy(src_row, k, t):
        return pltpu.make_async_copy(y_hbm.at[pl.ds(src_row, 1)], buf.at[k, pl.ds(t, 1)], sem)

    def issue(t, c):
        row_copy(pos_ref[2 * t], 0, t).start()
        row_copy(pos_ref[2 * t + 1], 1, t).start()
        return c

    lax.fori_loop(0, tc, issue, 0)

    def drain(t, c):
        row_copy(0, 0, 0).wait()
        row_copy(0, 1, 0).wait()
        return c

    lax.fori_loop(0, tc, drain, 0)

    rf = rf_ref[...]
    o_ref[...] = x_ref[...] + rf[:, 0:1] * buf[0] + rf[:, 1:2] * buf[1]


def _combine(x, rf, y, pos_flat, tc):
    t, d = x.shape
    return pl.pallas_call(
        functools.partial(_combine_kernel, tc=tc),
        grid=(t // tc,),
        in_specs=[pl.BlockSpec((2 * tc,), lambda i: (i,), memory_space=pltpu.SMEM),
                  pl.BlockSpec((tc, d), lambda i: (i, 0)),
                  pl.BlockSpec((tc, LANES), lambda i: (i, 0)),
                  pl.BlockSpec(memory_space=pl.ANY)],
        out_specs=pl.BlockSpec((tc, d), lambda i: (i, 0)),
        out_shape=jax.ShapeDtypeStruct((t, d), F32),
        scratch_shapes=[pltpu.VMEM((2, tc, d), F32), pltpu.SemaphoreType.DMA(())],
        compiler_params=_params(("arbitrary",)),
        name="moe_combine",
    )(pos_flat, x, rf, y)


def _rope_tables(s):
    pos = jnp.arange(s, dtype=F32)
    inv = ROPE_THETA ** (-jnp.arange(0, HEAD_DIM, 2, dtype=F32) / HEAD_DIM)
    ang = pos[:, None] * inv[None, :]
    cos, sin = jnp.cos(ang), jnp.sin(ang)
    cos_t = jnp.tile(cos, (1, LANES // (HEAD_DIM // 2)))
    sin_t = jnp.tile(jnp.concatenate([-sin, sin], axis=1), (1, LANES // HEAD_DIM))
    return cos_t, sin_t


def _dft_tables(n):
    k = jnp.arange(n, dtype=jnp.int32)
    ang = ((k[:, None] * k[None, :]) % n).astype(F32) * (2.0 * math.pi / n)
    scale = 1.0 / math.sqrt(n)
    return jnp.cos(ang) * scale, jnp.sin(ang) * scale


def _tiles(t, s):
    return dict(in_tm=min(1024, t), in_tn=512, tq=min(256, s), four_tm=min(512, s),
                out_tm=min(1024, t), out_tn=512, ffn_tm=min(512, t), ffn_tf=512,
                moe_tm=min(512, t), moe_tf=512, router_tm=min(512, t),
                gather_chunk=min(512, t), combine_tc=min(256, t))


def _moe_layer(x, g, w_router, wg, wu, wd, cfg):
    t, d = x.shape
    tm = cfg["moe_tm"]
    wr = jnp.pad(w_router, ((0, 0), (0, LANES - N_EXPERTS)))
    wr_hi = wr.astype(BF16)
    wr_lo = (wr - wr_hi.astype(F32)).astype(BF16)
    ri, rf, cnt = _router(x, g, wr_hi, wr_lo, cfg["router_tm"])

    n_tiles = (TOP_K * t) // tm + N_EXPERTS
    counts = cnt[0, :N_EXPERTS].astype(jnp.int32)
    padded = ((counts + tm - 1) // tm) * tm
    ends = jnp.cumsum(padded)
    offs = ends - padded
    pos = offs[ri[:, 0:TOP_K]] + ri[:, TOP_K:2 * TOP_K]
    pos_flat = pos.reshape(-1)
    tok = jnp.repeat(jnp.arange(t, dtype=jnp.int32), TOP_K)
    src = jnp.zeros((n_tiles * tm,), jnp.int32).at[pos_flat].set(tok)
    tile_start = jnp.arange(n_tiles, dtype=jnp.int32) * tm
    eid = jnp.minimum(jnp.sum(tile_start[:, None] >= ends[None, :], axis=1), N_EXPERTS - 1).astype(jnp.int32)
    nvalid = (ends[-1:] // tm).astype(jnp.int32)

    xs = _gather_rows(x, src, cfg["gather_chunk"])
    ys = _ffn(xs, g, wg, wu, wd, eid, nvalid, tm, cfg["moe_tf"], residual=False)
    return _combine(x, rf, ys, pos_flat, cfg["combine_tc"])


def _trunk(x3, attn_norm_g, w_in, q_norm_g, k_norm_g, lambda_q1, lambda_k1, lambda_q2, lambda_k2,
           subln_g, fourier_norm_g, w_out, ffn_norm_g, dense_w_gate, dense_w_up, dense_w_down,
           router_w, moe_w_gate, moe_w_up, moe_w_down):
    b, s, d = x3.shape
    t = b * s
    depth = w_in.shape[0]
    cfg = _tiles(t, s)
    x = x3.reshape(t, d)

    cos_t, sin_t = _rope_tables(s)
    cs_c, cs_s = _dft_tables(s)
    cs = jnp.concatenate([cs_c, -cs_s], axis=1).astype(BF16)
    cc_c, cc_s = _dft_tables(FOURIER_GROUP)
    cc = jnp.concatenate([cc_c, cc_s], axis=1).astype(BF16)

    dense_tiles = t // cfg["ffn_tm"]
    dense_eid = jnp.zeros((dense_tiles,), jnp.int32)
    dense_nv = jnp.full((1,), dense_tiles, jnp.int32)
    rep = LANES // HEAD_DIM

    for l in range(depth):
        lam_init = 0.8 - 0.6 * math.exp(-0.3 * l)
        proj = _in_proj(x, attn_norm_g[l][None, :], w_in[l].astype(BF16), cfg["in_tm"], cfg["in_tn"])
        lamp = jnp.stack([lambda_q1[l], lambda_k1[l], lambda_q2[l], lambda_k2[l]])
        a = _attention(proj, lamp, cos_t, sin_t, jnp.tile(q_norm_g[l], rep)[None, :],
                       jnp.tile(k_norm_g[l], rep)[None, :], subln_g[l][None, :], b, s, cfg["tq"], lam_init)
        f = _fourier(proj, cs, cc, fourier_norm_g[l][None, :], b, s, cfg["four_tm"])
        x = _out_proj(x, a, f, w_out[l].astype(BF16), cfg["out_tm"], cfg["out_tn"])
        gf = ffn_norm_g[l][None, :]
        i = l // 2
        if l % 2 == 0:
            x = _ffn(x, gf, dense_w_gate[i].astype(BF16)[None], dense_w_up[i].astype(BF16)[None],
                     dense_w_down[i].astype(BF16)[None], dense_eid, dense_nv,
                     cfg["ffn_tm"], cfg["ffn_tf"], residual=True)
        else:
            x = _moe_layer(x, gf, router_w[i], moe_w_gate[i].astype(BF16), moe_w_up[i].astype(BF16),
                           moe_w_down[i].astype(BF16), cfg)
    return x.reshape(b, s, d)


def kernel(x_prompt, x_sample, attn_norm_g, w_in, q_norm_g, k_norm_g, lambda_q1, lambda_k1, lambda_q2, lambda_k2, subln_g, fourier_norm_g, w_out, ffn_norm_g, dense_w_gate, dense_w_up, dense_w_down, router_w, moe_w_gate, moe_w_up, moe_w_down):
    nb = x_prompt.shape[0]
    x = jnp.concatenate([x_prompt, x_sample], axis=0)
    y = _trunk(x, attn_norm_g, w_in, q_norm_g, k_norm_g, lambda_q1, lambda_k1, lambda_q2, lambda_k2,
               subln_g, fourier_norm_g, w_out, ffn_norm_g, dense_w_gate, dense_w_up, dense_w_down,
               router_w, moe_w_gate, moe_w_up, moe_w_down)
    return (y[:nb], y[nb:])
```

```python
import functools
import math

import jax
import jax.numpy as jnp
from jax import lax
from jax.experimental import pallas as pl
from jax.experimental.pallas import tpu as pltpu

D_MODEL = 2048
ATTN_WIDTH = 1024
FOURIER_WIDTH = 1024
N_HEADS = 8
HEAD_DIM = 64
V_DIM = 128
N_FOURIER_GROUPS = 4
FOURIER_GROUP = 256
IN_WIDTH = 4096
N_EXPERTS = 8
TOP_K = 2
ROPE_THETA = 10000.0
EPS = 1e-6
SCORE_SCALE = HEAD_DIM ** -0.5
LOG2E = math.log2(math.e)
ATTN_SUB = 256

LANES = 128
VMEM_LIMIT = 56 * 1024 * 1024

F32 = jnp.float32
BF16 = jnp.bfloat16


def _params(sem, vmem=VMEM_LIMIT):
    return pltpu.CompilerParams(dimension_semantics=sem, vmem_limit_bytes=vmem)


def _in_proj_kernel(x_ref, g_ref, w_ref, o_ref, xn_ref):
    @pl.when(pl.program_id(1) == 0)
    def _():
        x = x_ref[...]
        ms = jnp.mean(x * x, axis=-1, keepdims=True)
        xn_ref[...] = (x * lax.rsqrt(ms + EPS) * g_ref[...]).astype(BF16)

    o_ref[...] = jnp.dot(xn_ref[...], w_ref[...], preferred_element_type=F32).astype(o_ref.dtype)


def _in_proj(x, g, w, tm, tn):
    t, d = x.shape
    n = w.shape[1]
    return pl.pallas_call(
        _in_proj_kernel,
        grid=(t // tm, n // tn),
        in_specs=[pl.BlockSpec((tm, d), lambda i, j: (i, 0)),
                  pl.BlockSpec((1, d), lambda i, j: (0, 0)),
                  pl.BlockSpec((d, tn), lambda i, j: (0, j))],
        out_specs=pl.BlockSpec((tm, tn), lambda i, j: (i, j)),
        out_shape=jax.ShapeDtypeStruct((t, n), BF16),
        scratch_shapes=[pltpu.VMEM((tm, d), BF16)],
        compiler_params=_params(("parallel", "arbitrary")),
        name="in_proj",
    )(x, g, w)


def _attn_kernel(lamp_ref, q_ref, k_ref, v_ref, cq_ref, sq_ref, ck_ref, sk_ref,
                 qg_ref, kg_ref, sg_ref, o_ref, kbuf, vbuf, *, lam_init):
    lane = lax.broadcasted_iota(jnp.int32, (1, LANES), 1)
    lo = lane < HEAD_DIM
    first = (lane % HEAD_DIM) < (HEAD_DIM // 2)

    def norm_rope(x, g, c, s):
        x2 = x * x
        ss_lo = jnp.sum(jnp.where(lo, x2, 0.0), axis=-1, keepdims=True)
        ss_hi = jnp.sum(jnp.where(lo, 0.0, x2), axis=-1, keepdims=True)
        ms = jnp.where(lo, ss_lo, ss_hi) * (1.0 / HEAD_DIM)
        y = x * lax.rsqrt(ms + EPS) * g
        partner = jnp.where(first, pltpu.roll(y, LANES - HEAD_DIM // 2, 1),
                            pltpu.roll(y, HEAD_DIM // 2, 1))
        return y * c + partner * s

    @pl.when(pl.program_id(2) == 0)
    def _():
        kbuf[...] = norm_rope(k_ref[...].astype(F32), kg_ref[...], ck_ref[...], sk_ref[...]).astype(BF16)
        vbuf[:, :V_DIM] = v_ref[...]
        vbuf[:, V_DIM:] = jnp.ones(v_ref.shape, BF16)

    lp = lamp_ref[...]
    a1 = jnp.sum(lp[0:1] * lp[1:2], axis=-1, keepdims=True)
    a2 = jnp.sum(lp[2:3] * lp[3:4], axis=-1, keepdims=True)
    lam = jnp.exp(a1) - jnp.exp(a2) + lam_init

    q = norm_rope(q_ref[...].astype(F32), qg_ref[...], cq_ref[...], sq_ref[...]) * (SCORE_SCALE * LOG2E)
    q0 = jnp.where(lo, q, 0.0).astype(BF16)
    q1 = jnp.where(lo, 0.0, q).astype(BF16)
    k = kbuf[...]
    v = vbuf[...]
    nt = (((1,), (1,)), ((), ()))
    for sub in range(q_ref.shape[0] // ATTN_SUB):
        rows = slice(sub * ATTN_SUB, (sub + 1) * ATTN_SUB)
        s0 = lax.dot_general(q0[rows], k, nt, preferred_element_type=F32)
        s1 = lax.dot_general(q1[rows], k, nt, preferred_element_type=F32)
        e0 = jnp.exp2(s0 - jnp.max(s0, axis=-1, keepdims=True)).astype(BF16)
        e1 = jnp.exp2(s1 - jnp.max(s1, axis=-1, keepdims=True)).astype(BF16)
        ov0 = jnp.dot(e0, v, preferred_element_type=F32)
        ov1 = jnp.dot(e1, v, preferred_element_type=F32)
        r0 = 1.0 / ov0[:, V_DIM:V_DIM + 1]
        r1 = lam / ov1[:, V_DIM:V_DIM + 1]
        o = ov0[:, :V_DIM] * r0 - ov1[:, :V_DIM] * r1
        ms = jnp.mean(o * o, axis=-1, keepdims=True)
        o_ref[rows, :] = (o * lax.rsqrt(ms + EPS) * sg_ref[...] * (1.0 - lam_init)).astype(o_ref.dtype)


def _attention(proj, lamp, cos_t, sin_t, qg, kg, sg, b, s, tq, lam_init):
    nq = s // tq
    row = lambda bi, h, qi: (bi * nq + qi, h)
    const = lambda bi, h, qi: (0, 0)
    return pl.pallas_call(
        functools.partial(_attn_kernel, lam_init=lam_init),
        grid=(b, N_HEADS, nq),
        in_specs=[pl.BlockSpec((4, HEAD_DIM), const),
                  pl.BlockSpec((tq, LANES), row),
                  pl.BlockSpec((s, LANES), lambda bi, h, qi: (bi, N_HEADS + h)),
                  pl.BlockSpec((s, LANES), lambda bi, h, qi: (bi, 2 * N_HEADS + h)),
                  pl.BlockSpec((tq, LANES), lambda bi, h, qi: (qi, 0)),
                  pl.BlockSpec((tq, LANES), lambda bi, h, qi: (qi, 0)),
                  pl.BlockSpec((s, LANES), const),
                  pl.BlockSpec((s, LANES), const),
                  pl.BlockSpec((1, LANES), const),
                  pl.BlockSpec((1, LANES), const),
                  pl.BlockSpec((1, LANES), const)],
        out_specs=pl.BlockSpec((tq, LANES), row),
        out_shape=jax.ShapeDtypeStruct((b * s, ATTN_WIDTH), BF16),
        scratch_shapes=[pltpu.VMEM((s, LANES), BF16), pltpu.VMEM((s, 2 * V_DIM), BF16)],
        compiler_params=_params(("parallel", "parallel", "arbitrary")),
        name="diff_attn",
    )(lamp, proj, proj, proj, cos_t, sin_t, cos_t, sin_t, qg, kg, sg)


def _fourier_kernel(u_ref, cs_ref, cc_ref, g_ref, o_ref, pq, *, s):
    fg = FOURIER_GROUP

    @pl.when(pl.program_id(1) == 0)
    def _():
        for gi in range(N_FOURIER_GROUPS):
            r = jnp.dot(u_ref[:, gi * fg:(gi + 1) * fg], cc_ref[...], preferred_element_type=F32)
            pq[0:s, gi * fg:(gi + 1) * fg] = r[:, :fg].astype(BF16)
            pq[s:2 * s, gi * fg:(gi + 1) * fg] = r[:, fg:].astype(BF16)

    f = jnp.dot(cs_ref[...], pq[...], preferred_element_type=F32)
    for gi in range(N_FOURIER_GROUPS):
        fgi = f[:, gi * fg:(gi + 1) * fg]
        ms = jnp.mean(fgi * fgi, axis=-1, keepdims=True)
        o_ref[:, gi * fg:(gi + 1) * fg] = (
            fgi * lax.rsqrt(ms + EPS) * g_ref[:, gi * fg:(gi + 1) * fg]).astype(o_ref.dtype)


def _fourier(proj, cs, cc, g, b, s, tm):
    n = s // tm
    return pl.pallas_call(
        functools.partial(_fourier_kernel, s=s),
        grid=(b, n),
        in_specs=[pl.BlockSpec((s, FOURIER_WIDTH), lambda bi, i: (bi, 3)),
                  pl.BlockSpec((tm, 2 * s), lambda bi, i: (i, 0)),
                  pl.BlockSpec((FOURIER_GROUP, 2 * FOURIER_GROUP), lambda bi, i: (0, 0)),
                  pl.BlockSpec((1, FOURIER_WIDTH), lambda bi, i: (0, 0))],
        out_specs=pl.BlockSpec((tm, FOURIER_WIDTH), lambda bi, i: (bi * n + i, 0)),
        out_shape=jax.ShapeDtypeStruct((b * s, FOURIER_WIDTH), BF16),
        scratch_shapes=[pltpu.VMEM((2 * s, FOURIER_WIDTH), BF16)],
        compiler_params=_params(("parallel", "arbitrary")),
        name="fourier_mix",
    )(proj, cs, cc, g)


def _out_proj_kernel(x_ref, a_ref, f_ref, w1_ref, w2_ref, o_ref):
    o_ref[...] = (x_ref[...]
                  + jnp.dot(a_ref[...], w1_ref[...], preferred_element_type=F32)
                  + jnp.dot(f_ref[...], w2_ref[...], preferred_element_type=F32))


def _out_proj(x, a, f, w, tm, tn):
    t, d = x.shape
    return pl.pallas_call(
        _out_proj_kernel,
        grid=(t // tm, d // tn),
        in_specs=[pl.BlockSpec((tm, tn), lambda i, j: (i, j)),
                  pl.BlockSpec((tm, ATTN_WIDTH), lambda i, j: (i, 0)),
                  pl.BlockSpec((tm, FOURIER_WIDTH), lambda i, j: (i, 0)),
                  pl.BlockSpec((ATTN_WIDTH, tn), lambda i, j: (0, j)),
                  pl.BlockSpec((FOURIER_WIDTH, tn), lambda i, j: (1, j))],
        out_specs=pl.BlockSpec((tm, tn), lambda i, j: (i, j)),
        out_shape=jax.ShapeDtypeStruct((t, d), F32),
        compiler_params=_params(("parallel", "arbitrary")),
        name="out_proj",
    )(x, a, f, w, w)


def _swiglu_partial(xn, wg_ref, wu_ref, wd_ref):
    gate = jnp.dot(xn, wg_ref[...], preferred_element_type=F32)
    up = jnp.dot(xn, wu_ref[...], preferred_element_type=F32)
    h = (gate * (1.0 / (1.0 + jnp.exp(-gate))) * up).astype(BF16)
    return jnp.dot(h, wd_ref[...], preferred_element_type=F32)


def _ffn_dense_kernel(x_ref, g_ref, wg_ref, wu_ref, wd_ref, o_ref, xn_ref):
    @pl.when(pl.program_id(1) == 0)
    def _():
        x = x_ref[...]
        ms = jnp.mean(x * x, axis=-1, keepdims=True)
        xn_ref[...] = (x * lax.rsqrt(ms + EPS) * g_ref[...]).astype(BF16)
        o_ref[...] = x

    o_ref[...] += _swiglu_partial(xn_ref[...], wg_ref, wu_ref, wd_ref)


def _ffn_dense(x, g, wg, wu, wd, tm, tf):
    t, d = x.shape
    ff = wg.shape[1]
    return pl.pallas_call(
        _ffn_dense_kernel,
        grid=(t // tm, ff // tf),
        in_specs=[pl.BlockSpec((tm, d), lambda i, f: (i, 0)),
                  pl.BlockSpec((1, d), lambda i, f: (0, 0)),
                  pl.BlockSpec((d, tf), lambda i, f: (0, f)),
                  pl.BlockSpec((d, tf), lambda i, f: (0, f)),
                  pl.BlockSpec((tf, d), lambda i, f: (f, 0))],
        out_specs=pl.BlockSpec((tm, d), lambda i, f: (i, 0)),
        out_shape=jax.ShapeDtypeStruct((t, d), F32),
        scratch_shapes=[pltpu.VMEM((tm, d), BF16)],
        compiler_params=_params(("parallel", "arbitrary")),
        name="swiglu_res",
    )(x, g, wg, wu, wd)


def _ffn_grouped_kernel(eid_ref, nv_ref, x_ref, wg_ref, wu_ref, wd_ref, o_ref):
    i = pl.program_id(0)
    f = pl.program_id(1)

    @pl.when(f == 0)
    def _():
        o_ref[...] = jnp.zeros_like(o_ref)

    @pl.when(i < nv_ref[0])
    def _():
        o_ref[...] += _swiglu_partial(x_ref[...], wg_ref, wu_ref, wd_ref)


def _ffn_grouped(xs, wg, wu, wd, eid, nvalid, tm, tf):
    r, d = xs.shape
    nf = wg.shape[2] // tf
    last = nf - 1

    def tile(i, nv_ref):
        return jnp.minimum(i, nv_ref[0] - 1)

    def fsel(i, f, nv_ref):
        return jnp.where(i < nv_ref[0], f, last)

    def up_map(i, f, eid_ref, nv_ref):
        return (eid_ref[tile(i, nv_ref)], 0, fsel(i, f, nv_ref))

    def down_map(i, f, eid_ref, nv_ref):
        return (eid_ref[tile(i, nv_ref)], fsel(i, f, nv_ref), 0)

    grid_spec = pltpu.PrefetchScalarGridSpec(
        num_scalar_prefetch=2,
        grid=(r // tm, nf),
        in_specs=[pl.BlockSpec((tm, d), lambda i, f, e, n: (tile(i, n), 0)),
                  pl.BlockSpec((None, d, tf), up_map),
                  pl.BlockSpec((None, d, tf), up_map),
                  pl.BlockSpec((None, tf, d), down_map)],
        out_specs=pl.BlockSpec((tm, d), lambda i, f, e, n: (i, 0)),
    )
    return pl.pallas_call(
        _ffn_grouped_kernel,
        grid_spec=grid_spec,
        out_shape=jax.ShapeDtypeStruct((r, d), F32),
        compiler_params=_params(("arbitrary", "arbitrary")),
        name="swiglu_grouped",
    )(eid, nvalid, xs, wg, wu, wd)


def _router_kernel(x_ref, g_ref, wh_ref, wl_ref, ri_ref, rf_ref, cnt_ref, carry, *, tm):
    @pl.when(pl.program_id(0) == 0)
    def _():
        carry[...] = jnp.zeros_like(carry)

    x = x_ref[...]
    ms = jnp.mean(x * x, axis=-1, keepdims=True)
    xn = x * lax.rsqrt(ms + EPS) * g_ref[...]
    xh = xn.astype(BF16)
    xl = (xn - xh.astype(F32)).astype(BF16)
    wh = wh_ref[...]
    logits = (jnp.dot(xh, wh, preferred_element_type=F32)
              + jnp.dot(xl, wh, preferred_element_type=F32)
              + jnp.dot(xh, wl_ref[...], preferred_element_type=F32))

    lane = lax.broadcasted_iota(jnp.int32, (tm, LANES), 1).astype(F32)
    neg = jnp.float32(-jnp.inf)
    lg = jnp.where(lane < N_EXPERTS, logits, neg)
    m1 = jnp.max(lg, axis=-1, keepdims=True)
    i1 = jnp.min(jnp.where(lg == m1, lane, float(LANES)), axis=-1, keepdims=True)
    lg2 = jnp.where(lane == i1, neg, lg)
    m2 = jnp.max(lg2, axis=-1, keepdims=True)
    i2 = jnp.min(jnp.where(lg2 == m2, lane, float(LANES)), axis=-1, keepdims=True)
    e = jnp.exp(m2 - m1)
    g1 = 1.0 / (1.0 + e)
    g2 = e / (1.0 + e)

    sel1 = lane == i1
    sel2 = lane == i2
    onehot = jnp.where(sel1 | sel2, 1.0, 0.0)
    r_i = lax.broadcasted_iota(jnp.int32, (tm, tm), 0)
    c_i = lax.broadcasted_iota(jnp.int32, (tm, tm), 1)
    tri = jnp.where(r_i > c_i, 1.0, 0.0).astype(BF16)
    before = jnp.dot(tri, onehot.astype(BF16), preferred_element_type=F32) + carry[...]
    rank1 = jnp.sum(jnp.where(sel1, before, 0.0), axis=-1, keepdims=True)
    rank2 = jnp.sum(jnp.where(sel2, before, 0.0), axis=-1, keepdims=True)
    carry[...] = carry[...] + jnp.sum(onehot, axis=0, keepdims=True)
    cnt_ref[...] = carry[...]

    ri = jnp.where(lane == 0.0, i1, jnp.where(lane == 1.0, i2,
         jnp.where(lane == 2.0, rank1, jnp.where(lane == 3.0, rank2, 0.0))))
    ri_ref[...] = ri.astype(jnp.int32)
    rf_ref[...] = jnp.where(lane == 0.0, g1, jnp.where(lane == 1.0, g2, 0.0))


def _router(x, g, wh, wl, tm):
    t, d = x.shape
    return pl.pallas_call(
        functools.partial(_router_kernel, tm=tm),
        grid=(t // tm,),
        in_specs=[pl.BlockSpec((tm, d), lambda i: (i, 0)),
                  pl.BlockSpec((1, d), lambda i: (0, 0)),
                  pl.BlockSpec((d, LANES), lambda i: (0, 0)),
                  pl.BlockSpec((d, LANES), lambda i: (0, 0))],
        out_specs=[pl.BlockSpec((tm, LANES), lambda i: (i, 0)),
                   pl.BlockSpec((tm, LANES), lambda i: (i, 0)),
                   pl.BlockSpec((1, LANES), lambda i: (0, 0))],
        out_shape=[jax.ShapeDtypeStruct((t, LANES), jnp.int32),
                   jax.ShapeDtypeStruct((t, LANES), F32),
                   jax.ShapeDtypeStruct((1, LANES), F32)],
        scratch_shapes=[pltpu.VMEM((1, LANES), F32)],
        compiler_params=_params(("arbitrary",)),
        name="router",
    )(x, g, wh, wl)


def _dispatch_kernel(src_ref, g_ref, x_hbm, o_ref, buf, sem, *, chunk):
    def issue(j, c):
        pltpu.make_async_copy(x_hbm.at[pl.ds(src_ref[j], 1)], buf.at[pl.ds(j, 1)], sem).start()
        return c

    lax.fori_loop(0, chunk, issue, 0, unroll=8)
    pltpu.make_async_copy(x_hbm.at[pl.ds(0, chunk)], buf, sem).wait()
    x = buf[...]
    ms = jnp.mean(x * x, axis=-1, keepdims=True)
    o_ref[...] = (x * lax.rsqrt(ms + EPS) * g_ref[...]).astype(o_ref.dtype)


def _dispatch(x, g, src, chunk):
    r = src.shape[0]
    d = x.shape[1]
    return pl.pallas_call(
        functools.partial(_dispatch_kernel, chunk=chunk),
        grid=(r // chunk,),
        in_specs=[pl.BlockSpec((chunk,), lambda i: (i,), memory_space=pltpu.SMEM),
                  pl.BlockSpec((1, d), lambda i: (0, 0)),
                  pl.BlockSpec(memory_space=pl.ANY)],
        out_specs=pl.BlockSpec((chunk, d), lambda i: (i, 0)),
        out_shape=jax.ShapeDtypeStruct((r, d), BF16),
        scratch_shapes=[pltpu.VMEM((chunk, d), F32), pltpu.SemaphoreType.DMA(())],
        compiler_params=_params(("arbitrary",)),
        name="moe_dispatch",
    )(src, g, x)


def _combine_kernel(pos_ref, x_ref, rf_ref, y_hbm, *rest, tc, split_tile):
    outs, (buf, sem) = rest[:-2], rest[-2:]

    def issue(t, c):
        for k in range(TOP_K):
            pltpu.make_async_copy(y_hbm.at[pl.ds(pos_ref[TOP_K * t + k], 1)], buf.at[k, pl.ds(t, 1)], sem).start()
        return c

    lax.fori_loop(0, tc, issue, 0, unroll=4)
    for k in range(TOP_K):
        pltpu.make_async_copy(y_hbm.at[pl.ds(0, tc)], buf.at[k], sem).wait()

    rf = rf_ref[...]
    res = x_ref[...] + rf[:, 0:1] * buf[0] + rf[:, 1:2] * buf[1]
    if split_tile is None:
        outs[0][...] = res
    else:
        @pl.when(pl.program_id(0) < split_tile)
        def _():
            outs[0][...] = res

        @pl.when(pl.program_id(0) >= split_tile)
        def _():
            outs[1][...] = res


def _combine(x, rf, y, pos_flat, tc, split_rows=None):
    t, d = x.shape
    if split_rows is None:
        split_tile = None
        out_specs = pl.BlockSpec((tc, d), lambda i: (i, 0))
        out_shape = jax.ShapeDtypeStruct((t, d), F32)
    else:
        split_tile = split_rows // tc
        out_specs = [pl.BlockSpec((tc, d), lambda i: (jnp.minimum(i, split_tile - 1), 0)),
                     pl.BlockSpec((tc, d), lambda i: (jnp.maximum(i - split_tile, 0), 0))]
        out_shape = [jax.ShapeDtypeStruct((split_rows, d), F32), jax.ShapeDtypeStruct((t - split_rows, d), F32)]
    return pl.pallas_call(
        functools.partial(_combine_kernel, tc=tc, split_tile=split_tile),
        grid=(t // tc,),
        in_specs=[pl.BlockSpec((TOP_K * tc,), lambda i: (i,), memory_space=pltpu.SMEM),
                  pl.BlockSpec((tc, d), lambda i: (i, 0)),
                  pl.BlockSpec((tc, LANES), lambda i: (i, 0)),
                  pl.BlockSpec(memory_space=pl.ANY)],
        out_specs=out_specs,
        out_shape=out_shape,
        scratch_shapes=[pltpu.VMEM((TOP_K, tc, d), F32), pltpu.SemaphoreType.DMA(())],
        compiler_params=_params(("arbitrary",)),
        name="moe_combine",
    )(pos_flat, x, rf, y)


def _rope_tables(s):
    pos = jnp.arange(s, dtype=F32)
    inv = ROPE_THETA ** (-jnp.arange(0, HEAD_DIM, 2, dtype=F32) / HEAD_DIM)
    ang = pos[:, None] * inv[None, :]
    cos, sin = jnp.cos(ang), jnp.sin(ang)
    cos_t = jnp.tile(cos, (1, LANES // (HEAD_DIM // 2)))
    sin_t = jnp.tile(jnp.concatenate([-sin, sin], axis=1), (1, LANES // HEAD_DIM))
    return cos_t, sin_t


def _dft_tables(n):
    k = jnp.arange(n, dtype=jnp.int32)
    ang = ((k[:, None] * k[None, :]) % n).astype(F32) * (2.0 * math.pi / n)
    scale = 1.0 / math.sqrt(n)
    return jnp.cos(ang) * scale, jnp.sin(ang) * scale


def _tiles(t, s):
    return dict(in_tm=min(1024, t), in_tn=1024, tq=min(1024, s), four_tm=min(512, s),
                out_tm=min(1024, t), out_tn=1024, ffn_tm=min(512, t), ffn_tf=512,
                moe_tm=min(512, t), moe_tf=1024, router_tm=min(512, t),
                gather_chunk=min(512, t), combine_tc=min(256, t))


def _moe_layer(x, g, w_router, wg, wu, wd, cfg, split_rows=None):
    t, d = x.shape
    tm = cfg["moe_tm"]
    wr = jnp.pad(w_router, ((0, 0), (0, LANES - N_EXPERTS)))
    wr_hi = wr.astype(BF16)
    wr_lo = (wr - wr_hi.astype(F32)).astype(BF16)
    ri, rf, cnt = _router(x, g, wr_hi, wr_lo, cfg["router_tm"])

    n_tiles = (TOP_K * t) // tm + N_EXPERTS
    counts = cnt[0, :N_EXPERTS].astype(jnp.int32)
    padded = ((counts + tm - 1) // tm) * tm
    ends = jnp.cumsum(padded)
    offs = ends - padded
    pos = offs[ri[:, 0:TOP_K]] + ri[:, TOP_K:2 * TOP_K]
    pos_flat = pos.reshape(-1)
    tok = jnp.repeat(jnp.arange(t, dtype=jnp.int32), TOP_K)
    src = jnp.zeros((n_tiles * tm,), jnp.int32).at[pos_flat].set(tok)
    tile_start = jnp.arange(n_tiles, dtype=jnp.int32) * tm
    eid = jnp.minimum(jnp.sum(tile_start[:, None] >= ends[None, :], axis=1), N_EXPERTS - 1).astype(jnp.int32)
    nvalid = (ends[-1:] // tm).astype(jnp.int32)

    xs = _dispatch(x, g, src, cfg["gather_chunk"])
    ys = _ffn_grouped(xs, wg, wu, wd, eid, nvalid, tm, cfg["moe_tf"])
    return _combine(x, rf, ys, pos_flat, cfg["combine_tc"], split_rows)


def _trunk(x3, split_batch, attn_norm_g, w_in, q_norm_g, k_norm_g, lambda_q1, lambda_k1, lambda_q2, lambda_k2,
           subln_g, fourier_norm_g, w_out, ffn_norm_g, dense_w_gate, dense_w_up, dense_w_down,
           router_w, moe_w_gate, moe_w_up, moe_w_down):
    b, s, d = x3.shape
    t = b * s
    depth = w_in.shape[0]
    cfg = _tiles(t, s)
    x = x3.reshape(t, d)

    cos_t, sin_t = _rope_tables(s)
    cs_c, cs_s = _dft_tables(s)
    cs = jnp.concatenate([cs_c, -cs_s], axis=1).astype(BF16)
    cc_c, cc_s = _dft_tables(FOURIER_GROUP)
    cc = jnp.concatenate([cc_c, cc_s], axis=1).astype(BF16)

    rep = LANES // HEAD_DIM
    split_rows = split_batch * s

    for l in range(depth):
        lam_init = 0.8 - 0.6 * math.exp(-0.3 * l)
        proj = _in_proj(x, attn_norm_g[l][None, :], w_in[l].astype(BF16), cfg["in_tm"], cfg["in_tn"])
        lamp = jnp.stack([lambda_q1[l], lambda_k1[l], lambda_q2[l], lambda_k2[l]])
        a = _attention(proj, lamp, cos_t, sin_t, jnp.tile(q_norm_g[l], rep)[None, :],
                       jnp.tile(k_norm_g[l], rep)[None, :], subln_g[l][None, :], b, s, cfg["tq"], lam_init)
        f = _fourier(proj, cs, cc, fourier_norm_g[l][None, :], b, s, cfg["four_tm"])
        x = _out_proj(x, a, f, w_out[l].astype(BF16), cfg["out_tm"], cfg["out_tn"])
        gf = ffn_norm_g[l][None, :]
        i = l // 2
        if l % 2 == 0:
            x = _ffn_dense(x, gf, dense_w_gate[i].astype(BF16), dense_w_up[i].astype(BF16),
                           dense_w_down[i].astype(BF16), cfg["ffn_tm"], cfg["ffn_tf"])
        else:
            last = l == depth - 1
            x = _moe_layer(x, gf, router_w[i], moe_w_gate[i].astype(BF16), moe_w_up[i].astype(BF16),
                           moe_w_down[i].astype(BF16), cfg, split_rows if last else None)
    if isinstance(x, (list, tuple)):
        y0, y1 = x
    else:
        y0, y1 = x[:split_rows], x[split_rows:]
    return y0.reshape(split_batch, s, d), y1.reshape(b - split_batch, s, d)


def kernel(x_prompt, x_sample, attn_norm_g, w_in, q_norm_g, k_norm_g, lambda_q1, lambda_k1, lambda_q2, lambda_k2, subln_g, fourier_norm_g, w_out, ffn_norm_g, dense_w_gate, dense_w_up, dense_w_down, router_w, moe_w_gate, moe_w_up, moe_w_down):
    nb = x_prompt.shape[0]
    x = jnp.concatenate([x_prompt, x_sample], axis=0)
    return _trunk(x, nb, attn_norm_g, w_in, q_norm_g, k_norm_g, lambda_q1, lambda_k1, lambda_q2, lambda_k2,
                  subln_g, fourier_norm_g, w_out, ffn_norm_g, dense_w_gate, dense_w_up, dense_w_down,
                  router_w, moe_w_gate, moe_w_up, moe_w_down)
```

```python
import functools
import math

import jax
import jax.numpy as jnp
from jax import lax
from jax.experimental import pallas as pl
from jax.experimental.pallas import tpu as pltpu

D_MODEL = 2048
ATTN_WIDTH = 1024
FOURIER_WIDTH = 1024
N_HEADS = 8
HEAD_DIM = 64
V_DIM = 128
N_FOURIER_GROUPS = 4
FOURIER_GROUP = 256
IN_WIDTH = 4096
N_EXPERTS = 8
TOP_K = 2
ROPE_THETA = 10000.0
EPS = 1e-6
SCORE_SCALE = HEAD_DIM ** -0.5
LOG2E = math.log2(math.e)
ATTN_SUB = 256

LANES = 128
VMEM_LIMIT = 56 * 1024 * 1024

F32 = jnp.float32
BF16 = jnp.bfloat16


def _params(sem, vmem=VMEM_LIMIT):
    return pltpu.CompilerParams(dimension_semantics=sem, vmem_limit_bytes=vmem)


def _in_proj_kernel(x_ref, g_ref, w_ref, o_ref, xn_ref):
    @pl.when(pl.program_id(1) == 0)
    def _():
        x = x_ref[...]
        ms = jnp.mean(x * x, axis=-1, keepdims=True)
        xn_ref[...] = (x * lax.rsqrt(ms + EPS) * g_ref[...]).astype(BF16)

    o_ref[...] = jnp.dot(xn_ref[...], w_ref[...], preferred_element_type=F32).astype(o_ref.dtype)


def _in_proj(x, g, w, layer, tm, tn):
    t, d = x.shape
    n = w.shape[2]
    return pl.pallas_call(
        _in_proj_kernel,
        grid=(t // tm, n // tn),
        in_specs=[pl.BlockSpec((tm, d), lambda i, j: (i, 0)),
                  pl.BlockSpec((1, d), lambda i, j: (0, 0)),
                  pl.BlockSpec((None, d, tn), lambda i, j: (layer, 0, j))],
        out_specs=pl.BlockSpec((tm, tn), lambda i, j: (i, j)),
        out_shape=jax.ShapeDtypeStruct((t, n), BF16),
        scratch_shapes=[pltpu.VMEM((tm, d), BF16)],
        compiler_params=_params(("parallel", "arbitrary")),
        name="in_proj",
    )(x, g, w)


def _attn_kernel(lamp_ref, q_ref, k_ref, v_ref, cq_ref, sq_ref, ck_ref, sk_ref,
                 qg_ref, kg_ref, sg_ref, o_ref, kbuf, vbuf, *, lam_init):
    lane = lax.broadcasted_iota(jnp.int32, (1, LANES), 1)
    lo = lane < HEAD_DIM
    first = (lane % HEAD_DIM) < (HEAD_DIM // 2)

    def norm_rope(x, g, c, s):
        x2 = x * x
        ss_lo = jnp.sum(jnp.where(lo, x2, 0.0), axis=-1, keepdims=True)
        ss_hi = jnp.sum(jnp.where(lo, 0.0, x2), axis=-1, keepdims=True)
        ms = jnp.where(lo, ss_lo, ss_hi) * (1.0 / HEAD_DIM)
        y = x * lax.rsqrt(ms + EPS) * g
        partner = jnp.where(first, pltpu.roll(y, LANES - HEAD_DIM // 2, 1),
                            pltpu.roll(y, HEAD_DIM // 2, 1))
        return y * c + partner * s

    @pl.when(pl.program_id(2) == 0)
    def _():
        kbuf[...] = norm_rope(k_ref[...].astype(F32), kg_ref[...], ck_ref[...], sk_ref[...]).astype(BF16)
        vbuf[:, :V_DIM] = v_ref[...]
        vbuf[:, V_DIM:] = jnp.ones(v_ref.shape, BF16)

    lp = lamp_ref[...]
    a1 = jnp.sum(lp[0:1] * lp[1:2], axis=-1, keepdims=True)
    a2 = jnp.sum(lp[2:3] * lp[3:4], axis=-1, keepdims=True)
    lam = jnp.exp(a1) - jnp.exp(a2) + lam_init

    q = norm_rope(q_ref[...].astype(F32), qg_ref[...], cq_ref[...], sq_ref[...]) * (SCORE_SCALE * LOG2E)
    q0 = jnp.where(lo, q, 0.0).astype(BF16)
    q1 = jnp.where(lo, 0.0, q).astype(BF16)
    k = kbuf[...]
    v = vbuf[...]
    nt = (((1,), (1,)), ((), ()))
    nsub = q_ref.shape[0] // ATTN_SUB

    def scores(sub):
        rows = slice(sub * ATTN_SUB, (sub + 1) * ATTN_SUB)
        return (lax.dot_general(q0[rows], k, nt, preferred_element_type=F32),
                lax.dot_general(q1[rows], k, nt, preferred_element_type=F32))

    ahead = scores(0)
    for sub in range(nsub):
        rows = slice(sub * ATTN_SUB, (sub + 1) * ATTN_SUB)
        s0, s1 = ahead
        if sub + 1 < nsub:
            ahead = scores(sub + 1)
        e0 = jnp.exp2(s0 - jnp.max(s0, axis=-1, keepdims=True)).astype(BF16)
        e1 = jnp.exp2(s1 - jnp.max(s1, axis=-1, keepdims=True)).astype(BF16)
        ov0 = jnp.dot(e0, v, preferred_element_type=F32)
        ov1 = jnp.dot(e1, v, preferred_element_type=F32)
        r0 = 1.0 / ov0[:, V_DIM:V_DIM + 1]
        r1 = lam / ov1[:, V_DIM:V_DIM + 1]
        o = ov0[:, :V_DIM] * r0 - ov1[:, :V_DIM] * r1
        ms = jnp.mean(o * o, axis=-1, keepdims=True)
        o_ref[rows, :] = (o * lax.rsqrt(ms + EPS) * sg_ref[...] * (1.0 - lam_init)).astype(o_ref.dtype)


def _attention(proj, lamp, cos_t, sin_t, qg, kg, sg, b, s, tq, lam_init):
    nq = s // tq
    row = lambda bi, h, qi: (bi * nq + qi, h)
    const = lambda bi, h, qi: (0, 0)
    return pl.pallas_call(
        functools.partial(_attn_kernel, lam_init=lam_init),
        grid=(b, N_HEADS, nq),
        in_specs=[pl.BlockSpec((4, HEAD_DIM), const),
                  pl.BlockSpec((tq, LANES), row),
                  pl.BlockSpec((s, LANES), lambda bi, h, qi: (bi, N_HEADS + h)),
                  pl.BlockSpec((s, LANES), lambda bi, h, qi: (bi, 2 * N_HEADS + h)),
                  pl.BlockSpec((tq, LANES), lambda bi, h, qi: (qi, 0)),
                  pl.BlockSpec((tq, LANES), lambda bi, h, qi: (qi, 0)),
                  pl.BlockSpec((s, LANES), const),
                  pl.BlockSpec((s, LANES), const),
                  pl.BlockSpec((1, LANES), const),
                  pl.BlockSpec((1, LANES), const),
                  pl.BlockSpec((1, LANES), const)],
        out_specs=pl.BlockSpec((tq, LANES), row),
        out_shape=jax.ShapeDtypeStruct((b * s, ATTN_WIDTH), BF16),
        scratch_shapes=[pltpu.VMEM((s, LANES), BF16), pltpu.VMEM((s, 2 * V_DIM), BF16)],
        compiler_params=_params(("parallel", "parallel", "arbitrary")),
        name="diff_attn",
    )(lamp, proj, proj, proj, cos_t, sin_t, cos_t, sin_t, qg, kg, sg)


def _fourier_kernel(u_ref, cs_ref, cc_ref, g_ref, o_ref, pq, *, s):
    fg = FOURIER_GROUP

    @pl.when(pl.program_id(1) == 0)
    def _():
        for gi in range(N_FOURIER_GROUPS):
            r = jnp.dot(u_ref[:, gi * fg:(gi + 1) * fg], cc_ref[...], preferred_element_type=F32)
            pq[0:s, gi * fg:(gi + 1) * fg] = r[:, :fg].astype(BF16)
            pq[s:2 * s, gi * fg:(gi + 1) * fg] = r[:, fg:].astype(BF16)

    f = jnp.dot(cs_ref[...], pq[...], preferred_element_type=F32)
    for gi in range(N_FOURIER_GROUPS):
        fgi = f[:, gi * fg:(gi + 1) * fg]
        ms = jnp.mean(fgi * fgi, axis=-1, keepdims=True)
        o_ref[:, gi * fg:(gi + 1) * fg] = (
            fgi * lax.rsqrt(ms + EPS) * g_ref[:, gi * fg:(gi + 1) * fg]).astype(o_ref.dtype)


def _fourier(proj, cs, cc, g, b, s, tm):
    n = s // tm
    return pl.pallas_call(
        functools.partial(_fourier_kernel, s=s),
        grid=(b, n),
        in_specs=[pl.BlockSpec((s, FOURIER_WIDTH), lambda bi, i: (bi, 3)),
                  pl.BlockSpec((tm, 2 * s), lambda bi, i: (i, 0)),
                  pl.BlockSpec((FOURIER_GROUP, 2 * FOURIER_GROUP), lambda bi, i: (0, 0)),
                  pl.BlockSpec((1, FOURIER_WIDTH), lambda bi, i: (0, 0))],
        out_specs=pl.BlockSpec((tm, FOURIER_WIDTH), lambda bi, i: (bi * n + i, 0)),
        out_shape=jax.ShapeDtypeStruct((b * s, FOURIER_WIDTH), BF16),
        scratch_shapes=[pltpu.VMEM((2 * s, FOURIER_WIDTH), BF16)],
        compiler_params=_params(("parallel", "arbitrary")),
        name="fourier_mix",
    )(proj, cs, cc, g)


def _out_proj_kernel(x_ref, a_ref, f_ref, w1_ref, w2_ref, o_ref):
    o_ref[...] = (x_ref[...]
                  + jnp.dot(a_ref[...], w1_ref[...], preferred_element_type=F32)
                  + jnp.dot(f_ref[...], w2_ref[...], preferred_element_type=F32))


def _out_proj(x, a, f, w, layer, tm, tn):
    t, d = x.shape
    return pl.pallas_call(
        _out_proj_kernel,
        grid=(t // tm, d // tn),
        in_specs=[pl.BlockSpec((tm, tn), lambda i, j: (i, j)),
                  pl.BlockSpec((tm, ATTN_WIDTH), lambda i, j: (i, 0)),
                  pl.BlockSpec((tm, FOURIER_WIDTH), lambda i, j: (i, 0)),
                  pl.BlockSpec((None, ATTN_WIDTH, tn), lambda i, j: (layer, 0, j)),
                  pl.BlockSpec((None, FOURIER_WIDTH, tn), lambda i, j: (layer, 1, j))],
        out_specs=pl.BlockSpec((tm, tn), lambda i, j: (i, j)),
        out_shape=jax.ShapeDtypeStruct((t, d), F32),
        compiler_params=_params(("parallel", "arbitrary")),
        name="out_proj",
    )(x, a, f, w, w)


def _swiglu_partial(xn, wg_ref, wu_ref, wd_ref):
    gate = jnp.dot(xn, wg_ref[...], preferred_element_type=F32)
    up = jnp.dot(xn, wu_ref[...], preferred_element_type=F32)
    h = (gate * (1.0 / (1.0 + jnp.exp(-gate))) * up).astype(BF16)
    return jnp.dot(h, wd_ref[...], preferred_element_type=F32)


def _ffn_dense_kernel(x_ref, g_ref, wg_ref, wu_ref, wd_ref, o_ref, xn_ref):
    @pl.when(pl.program_id(1) == 0)
    def _():
        x = x_ref[...]
        ms = jnp.mean(x * x, axis=-1, keepdims=True)
        xn_ref[...] = (x * lax.rsqrt(ms + EPS) * g_ref[...]).astype(BF16)
        o_ref[...] = x

    o_ref[...] += _swiglu_partial(xn_ref[...], wg_ref, wu_ref, wd_ref)


def _ffn_dense(x, g, wg, wu, wd, layer, tm, tf):
    t, d = x.shape
    ff = wg.shape[2]
    return pl.pallas_call(
        _ffn_dense_kernel,
        grid=(t // tm, ff // tf),
        in_specs=[pl.BlockSpec((tm, d), lambda i, f: (i, 0)),
                  pl.BlockSpec((1, d), lambda i, f: (0, 0)),
                  pl.BlockSpec((None, d, tf), lambda i, f: (layer, 0, f)),
                  pl.BlockSpec((None, d, tf), lambda i, f: (layer, 0, f)),
                  pl.BlockSpec((None, tf, d), lambda i, f: (layer, f, 0))],
        out_specs=pl.BlockSpec((tm, d), lambda i, f: (i, 0)),
        out_shape=jax.ShapeDtypeStruct((t, d), F32),
        scratch_shapes=[pltpu.VMEM((tm, d), BF16)],
        compiler_params=_params(("parallel", "arbitrary")),
        name="swiglu_res",
    )(x, g, wg, wu, wd)


def _ffn_grouped_kernel(eid_ref, nv_ref, x_ref, wg_ref, wu_ref, wd_ref, o_ref):
    i = pl.program_id(0)
    f = pl.program_id(1)

    @pl.when(f == 0)
    def _():
        o_ref[...] = jnp.zeros_like(o_ref)

    @pl.when(i < nv_ref[0])
    def _():
        o_ref[...] += _swiglu_partial(x_ref[...], wg_ref, wu_ref, wd_ref)


def _ffn_grouped(xs, wg, wu, wd, layer, eid, nvalid, tm, tf):
    r, d = xs.shape
    nf = wg.shape[3] // tf
    last = nf - 1

    def tile(i, nv_ref):
        return jnp.minimum(i, nv_ref[0] - 1)

    def fsel(i, f, nv_ref):
        return jnp.where(i < nv_ref[0], f, last)

    def up_map(i, f, eid_ref, nv_ref):
        return (layer, eid_ref[tile(i, nv_ref)], 0, fsel(i, f, nv_ref))

    def down_map(i, f, eid_ref, nv_ref):
        return (layer, eid_ref[tile(i, nv_ref)], fsel(i, f, nv_ref), 0)

    grid_spec = pltpu.PrefetchScalarGridSpec(
        num_scalar_prefetch=2,
        grid=(r // tm, nf),
        in_specs=[pl.BlockSpec((tm, d), lambda i, f, e, n: (tile(i, n), 0)),
                  pl.BlockSpec((None, None, d, tf), up_map),
                  pl.BlockSpec((None, None, d, tf), up_map),
                  pl.BlockSpec((None, None, tf, d), down_map)],
        out_specs=pl.BlockSpec((tm, d), lambda i, f, e, n: (i, 0)),
    )
    return pl.pallas_call(
        _ffn_grouped_kernel,
        grid_spec=grid_spec,
        out_shape=jax.ShapeDtypeStruct((r, d), F32),
        compiler_params=_params(("arbitrary", "arbitrary")),
        name="swiglu_grouped",
    )(eid, nvalid, xs, wg, wu, wd)


def _router_kernel(x_ref, g_ref, wh_ref, wl_ref, ri_ref, rf_ref, cnt_ref, carry, *, tm):
    @pl.when(pl.program_id(0) == 0)
    def _():
        carry[...] = jnp.zeros_like(carry)

    x = x_ref[...]
    ms = jnp.mean(x * x, axis=-1, keepdims=True)
    xn = x * lax.rsqrt(ms + EPS) * g_ref[...]
    xh = xn.astype(BF16)
    xl = (xn - xh.astype(F32)).astype(BF16)
    wh = wh_ref[...]
    logits = (jnp.dot(xh, wh, preferred_element_type=F32)
              + jnp.dot(xl, wh, preferred_element_type=F32)
              + jnp.dot(xh, wl_ref[...], preferred_element_type=F32))

    lane = lax.broadcasted_iota(jnp.int32, (tm, LANES), 1).astype(F32)
    neg = jnp.float32(-jnp.inf)
    lg = jnp.where(lane < N_EXPERTS, logits, neg)
    m1 = jnp.max(lg, axis=-1, keepdims=True)
    i1 = jnp.min(jnp.where(lg == m1, lane, float(LANES)), axis=-1, keepdims=True)
    lg2 = jnp.where(lane == i1, neg, lg)
    m2 = jnp.max(lg2, axis=-1, keepdims=True)
    i2 = jnp.min(jnp.where(lg2 == m2, lane, float(LANES)), axis=-1, keepdims=True)
    e = jnp.exp(m2 - m1)
    g1 = 1.0 / (1.0 + e)
    g2 = e / (1.0 + e)

    sel1 = lane == i1
    sel2 = lane == i2
    onehot = jnp.where(sel1 | sel2, 1.0, 0.0)
    r_i = lax.broadcasted_iota(jnp.int32, (tm, tm), 0)
    c_i = lax.broadcasted_iota(jnp.int32, (tm, tm), 1)
    tri = jnp.where(r_i > c_i, 1.0, 0.0).astype(BF16)
    before = jnp.dot(tri, onehot.astype(BF16), preferred_element_type=F32) + carry[...]
    rank1 = jnp.sum(jnp.where(sel1, before, 0.0), axis=-1, keepdims=True)
    rank2 = jnp.sum(jnp.where(sel2, before, 0.0), axis=-1, keepdims=True)
    carry[...] = carry[...] + jnp.sum(onehot, axis=0, keepdims=True)
    cnt_ref[...] = carry[...]

    ri = jnp.where(lane == 0.0, i1, jnp.where(lane == 1.0, i2,
         jnp.where(lane == 2.0, rank1, jnp.where(lane == 3.0, rank2, 0.0))))
    ri_ref[...] = ri.astype(jnp.int32)
    rf_ref[...] = jnp.where(lane == 0.0, g1, jnp.where(lane == 1.0, g2, 0.0))


def _router(x, g, wh, wl, tm):
    t, d = x.shape
    return pl.pallas_call(
        functools.partial(_router_kernel, tm=tm),
        grid=(t // tm,),
        in_specs=[pl.BlockSpec((tm, d), lambda i: (i, 0)),
                  pl.BlockSpec((1, d), lambda i: (0, 0)),
                  pl.BlockSpec((d, LANES), lambda i: (0, 0)),
                  pl.BlockSpec((d, LANES), lambda i: (0, 0))],
        out_specs=[pl.BlockSpec((tm, LANES), lambda i: (i, 0)),
                   pl.BlockSpec((tm, LANES), lambda i: (i, 0)),
                   pl.BlockSpec((1, LANES), lambda i: (0, 0))],
        out_shape=[jax.ShapeDtypeStruct((t, LANES), jnp.int32),
                   jax.ShapeDtypeStruct((t, LANES), F32),
                   jax.ShapeDtypeStruct((1, LANES), F32)],
        scratch_shapes=[pltpu.VMEM((1, LANES), F32)],
        compiler_params=_params(("arbitrary",)),
        name="router",
    )(x, g, wh, wl)


def _dispatch_kernel(src_ref, nxt_ref, g_ref, x_hbm, o_ref, buf, sem, *, chunk):
    i = pl.program_id(0)
    slot = i % 2

    def gather(idx_ref, s):
        def issue(j, c):
            pltpu.make_async_copy(x_hbm.at[pl.ds(idx_ref[j], 1)], buf.at[s, pl.ds(j, 1)], sem.at[s]).start()
            return c

        lax.fori_loop(0, chunk, issue, 0, unroll=8)

    @pl.when(i == 0)
    def _():
        gather(src_ref, slot)

    @pl.when(i + 1 < pl.num_programs(0))
    def _():
        gather(nxt_ref, 1 - slot)

    pltpu.make_async_copy(x_hbm.at[pl.ds(0, chunk)], buf.at[slot], sem.at[slot]).wait()
    x = buf[slot]
    ms = jnp.mean(x * x, axis=-1, keepdims=True)
    o_ref[...] = (x * lax.rsqrt(ms + EPS) * g_ref[...]).astype(o_ref.dtype)


def _dispatch(x, g, src, chunk):
    r = src.shape[0]
    d = x.shape[1]
    n = r // chunk
    return pl.pallas_call(
        functools.partial(_dispatch_kernel, chunk=chunk),
        grid=(n,),
        in_specs=[pl.BlockSpec((chunk,), lambda i: (i,), memory_space=pltpu.SMEM),
                  pl.BlockSpec((chunk,), lambda i: (jnp.minimum(i + 1, n - 1),), memory_space=pltpu.SMEM),
                  pl.BlockSpec((1, d), lambda i: (0, 0)),
                  pl.BlockSpec(memory_space=pl.ANY)],
        out_specs=pl.BlockSpec((chunk, d), lambda i: (i, 0)),
        out_shape=jax.ShapeDtypeStruct((r, d), BF16),
        scratch_shapes=[pltpu.VMEM((2, chunk, d), F32), pltpu.SemaphoreType.DMA((2,))],
        compiler_params=_params(("arbitrary",)),
        name="moe_dispatch",
    )(src, src, g, x)


def _combine_kernel(pos_ref, nxt_ref, x_ref, rf_ref, y_hbm, *rest, tc, split_tile):
    outs, (buf, sem) = rest[:-2], rest[-2:]
    i = pl.program_id(0)
    slot = i % 2

    def gather(idx_ref, s):
        def issue(t, c):
            for k in range(TOP_K):
                pltpu.make_async_copy(y_hbm.at[pl.ds(idx_ref[TOP_K * t + k], 1)],
                                      buf.at[s, k, pl.ds(t, 1)], sem.at[s]).start()
            return c

        lax.fori_loop(0, tc, issue, 0, unroll=4)

    @pl.when(i == 0)
    def _():
        gather(pos_ref, slot)

    @pl.when(i + 1 < pl.num_programs(0))
    def _():
        gather(nxt_ref, 1 - slot)

    for k in range(TOP_K):
        pltpu.make_async_copy(y_hbm.at[pl.ds(0, tc)], buf.at[slot, k], sem.at[slot]).wait()

    rf = rf_ref[...]
    res = x_ref[...] + rf[:, 0:1] * buf[slot, 0] + rf[:, 1:2] * buf[slot, 1]
    if split_tile is None:
        outs[0][...] = res
    else:
        @pl.when(pl.program_id(0) < split_tile)
        def _():
            outs[0][...] = res

        @pl.when(pl.program_id(0) >= split_tile)
        def _():
            outs[1][...] = res


def _combine(x, rf, y, pos_flat, tc, split_rows=None):
    t, d = x.shape
    if split_rows is None:
        split_tile = None
        out_specs = pl.BlockSpec((tc, d), lambda i: (i, 0))
        out_shape = jax.ShapeDtypeStruct((t, d), F32)
    else:
        split_tile = split_rows // tc
        out_specs = [pl.BlockSpec((tc, d), lambda i: (jnp.minimum(i, split_tile - 1), 0)),
                     pl.BlockSpec((tc, d), lambda i: (jnp.maximum(i - split_tile, 0), 0))]
        out_shape = [jax.ShapeDtypeStruct((split_rows, d), F32), jax.ShapeDtypeStruct((t - split_rows, d), F32)]
    n = t // tc
    return pl.pallas_call(
        functools.partial(_combine_kernel, tc=tc, split_tile=split_tile),
        grid=(n,),
        in_specs=[pl.BlockSpec((TOP_K * tc,), lambda i: (i,), memory_space=pltpu.SMEM),
                  pl.BlockSpec((TOP_K * tc,), lambda i: (jnp.minimum(i + 1, n - 1),), memory_space=pltpu.SMEM),
                  pl.BlockSpec((tc, d), lambda i: (i, 0)),
                  pl.BlockSpec((tc, LANES), lambda i: (i, 0)),
                  pl.BlockSpec(memory_space=pl.ANY)],
        out_specs=out_specs,
        out_shape=out_shape,
        scratch_shapes=[pltpu.VMEM((2, TOP_K, tc, d), F32), pltpu.SemaphoreType.DMA((2,))],
        compiler_params=_params(("arbitrary",)),
        name="moe_combine",
    )(pos_flat, pos_flat, x, rf, y)


def _rope_tables(s):
    pos = jnp.arange(s, dtype=F32)
    inv = ROPE_THETA ** (-jnp.arange(0, HEAD_DIM, 2, dtype=F32) / HEAD_DIM)
    ang = pos[:, None] * inv[None, :]
    cos, sin = jnp.cos(ang), jnp.sin(ang)
    cos_t = jnp.tile(cos, (1, LANES // (HEAD_DIM // 2)))
    sin_t = jnp.tile(jnp.concatenate([-sin, sin], axis=1), (1, LANES // HEAD_DIM))
    return cos_t, sin_t


def _dft_tables(n):
    k = jnp.arange(n, dtype=jnp.int32)
    ang = ((k[:, None] * k[None, :]) % n).astype(F32) * (2.0 * math.pi / n)
    scale = 1.0 / math.sqrt(n)
    return jnp.cos(ang) * scale, jnp.sin(ang) * scale


def _tiles(t, s):
    return dict(in_tm=min(1024, t), in_tn=1024, tq=min(2048, s), four_tm=min(512, s),
                out_tm=min(1024, t), out_tn=1024, ffn_tm=min(1024, t), ffn_tf=512,
                moe_tm=min(512, t), moe_tf=1024, router_tm=min(512, t),
                gather_chunk=min(512, t), combine_tc=min(256, t))


def _moe_layer(x, g, w_router, wg, wu, wd, layer, cfg, split_rows=None):
    t, d = x.shape
    tm = cfg["moe_tm"]
    wr = jnp.pad(w_router, ((0, 0), (0, LANES - N_EXPERTS)))
    wr_hi = wr.astype(BF16)
    wr_lo = (wr - wr_hi.astype(F32)).astype(BF16)
    ri, rf, cnt = _router(x, g, wr_hi, wr_lo, cfg["router_tm"])

    n_tiles = (TOP_K * t) // tm + N_EXPERTS
    counts = cnt[0, :N_EXPERTS].astype(jnp.int32)
    padded = ((counts + tm - 1) // tm) * tm
    ends = jnp.cumsum(padded)
    offs = ends - padded
    pos = offs[ri[:, 0:TOP_K]] + ri[:, TOP_K:2 * TOP_K]
    pos_flat = pos.reshape(-1)
    tok = jnp.repeat(jnp.arange(t, dtype=jnp.int32), TOP_K)
    src = jnp.zeros((n_tiles * tm,), jnp.int32).at[pos_flat].set(tok)
    tile_start = jnp.arange(n_tiles, dtype=jnp.int32) * tm
    eid = jnp.minimum(jnp.sum(tile_start[:, None] >= ends[None, :], axis=1), N_EXPERTS - 1).astype(jnp.int32)
    nvalid = (ends[-1:] // tm).astype(jnp.int32)

    xs = _dispatch(x, g, src, cfg["gather_chunk"])
    ys = _ffn_grouped(xs, wg, wu, wd, layer, eid, nvalid, tm, cfg["moe_tf"])
    return _combine(x, rf, ys, pos_flat, cfg["combine_tc"], split_rows)


def _trunk(x3, split_batch, attn_norm_g, w_in, q_norm_g, k_norm_g, lambda_q1, lambda_k1, lambda_q2, lambda_k2,
           subln_g, fourier_norm_g, w_out, ffn_norm_g, dense_w_gate, dense_w_up, dense_w_down,
           router_w, moe_w_gate, moe_w_up, moe_w_down):
    b, s, d = x3.shape
    t = b * s
    depth = w_in.shape[0]
    cfg = _tiles(t, s)
    x = x3.reshape(t, d)

    cos_t, sin_t = _rope_tables(s)
    cs_c, cs_s = _dft_tables(s)
    cs = jnp.concatenate([cs_c, -cs_s], axis=1).astype(BF16)
    cc_c, cc_s = _dft_tables(FOURIER_GROUP)
    cc = jnp.concatenate([cc_c, cc_s], axis=1).astype(BF16)

    rep = LANES // HEAD_DIM
    split_rows = split_batch * s
    w_in, w_out = w_in.astype(BF16), w_out.astype(BF16)
    dense_w = [w.astype(BF16) for w in (dense_w_gate, dense_w_up, dense_w_down)]
    moe_w = [w.astype(BF16) for w in (moe_w_gate, moe_w_up, moe_w_down)]

    for l in range(depth):
        lam_init = 0.8 - 0.6 * math.exp(-0.3 * l)
        proj = _in_proj(x, attn_norm_g[l][None, :], w_in, l, cfg["in_tm"], cfg["in_tn"])
        lamp = jnp.stack([lambda_q1[l], lambda_k1[l], lambda_q2[l], lambda_k2[l]])
        a = _attention(proj, lamp, cos_t, sin_t, jnp.tile(q_norm_g[l], rep)[None, :],
                       jnp.tile(k_norm_g[l], rep)[None, :], subln_g[l][None, :], b, s, cfg["tq"], lam_init)
        f = _fourier(proj, cs, cc, fourier_norm_g[l][None, :], b, s, cfg["four_tm"])
        x = _out_proj(x, a, f, w_out, l, cfg["out_tm"], cfg["out_tn"])
        gf = ffn_norm_g[l][None, :]
        i = l // 2
        if l % 2 == 0:
            x = _ffn_dense(x, gf, *dense_w, i, cfg["ffn_tm"], cfg["ffn_tf"])
        else:
            last = l == depth - 1
            x = _moe_layer(x, gf, router_w[i], *moe_w, i, cfg, split_rows if last else None)
    if isinstance(x, (list, tuple)):
        y0, y1 = x
    else:
        y0, y1 = x[:split_rows], x[split_rows:]
    return y0.reshape(split_batch, s, d), y1.reshape(b - split_batch, s, d)


def kernel(x_prompt, x_sample, attn_norm_g, w_in, q_norm_g, k_norm_g, lambda_q1, lambda_k1, lambda_q2, lambda_k2, subln_g, fourier_norm_g, w_out, ffn_norm_g, dense_w_gate, dense_w_up, dense_w_down, router_w, moe_w_gate, moe_w_up, moe_w_down):
    nb = x_prompt.shape[0]
    x = jnp.concatenate([x_prompt, x_sample], axis=0)
    return _trunk(x, nb, attn_norm_g, w_in, q_norm_g, k_norm_g, lambda_q1, lambda_k1, lambda_q2, lambda_k2,
                  subln_g, fourier_norm_g, w_out, ffn_norm_g, dense_w_gate, dense_w_up, dense_w_down,
                  router_w, moe_w_gate, moe_w_up, moe_w_down)
```

```python
import functools
import math

import jax
import jax.numpy as jnp
from jax import lax
from jax.experimental import pallas as pl
from jax.experimental.pallas import tpu as pltpu

D_MODEL = 2048
ATTN_WIDTH = 1024
FOURIER_WIDTH = 1024
N_HEADS = 8
HEAD_DIM = 64
V_DIM = 128
N_FOURIER_GROUPS = 4
FOURIER_GROUP = 256
IN_WIDTH = 4096
N_EXPERTS = 8
TOP_K = 2
ROPE_THETA = 10000.0
EPS = 1e-6
SCORE_SCALE = HEAD_DIM ** -0.5
LOG2E = math.log2(math.e)
ATTN_SUB = 256

LANES = 128
VMEM_LIMIT = 56 * 1024 * 1024

F32 = jnp.float32
BF16 = jnp.bfloat16


def _params(sem, vmem=VMEM_LIMIT):
    return pltpu.CompilerParams(dimension_semantics=sem, vmem_limit_bytes=vmem)


def _in_proj_kernel(x_ref, g_ref, w_ref, o_ref, xn_ref):
    @pl.when(pl.program_id(1) == 0)
    def _():
        x = x_ref[...]
        ms = jnp.mean(x * x, axis=-1, keepdims=True)
        xn_ref[...] = (x * lax.rsqrt(ms + EPS) * g_ref[...]).astype(BF16)

    o_ref[...] = jnp.dot(xn_ref[...], w_ref[...], preferred_element_type=F32).astype(o_ref.dtype)


def _in_proj(x, g, w, layer, tm, tn):
    t, d = x.shape
    n = w.shape[2]
    return pl.pallas_call(
        _in_proj_kernel,
        grid=(t // tm, n // tn),
        in_specs=[pl.BlockSpec((tm, d), lambda i, j: (i, 0)),
                  pl.BlockSpec((1, d), lambda i, j: (0, 0)),
                  pl.BlockSpec((None, d, tn), lambda i, j: (layer, 0, j))],
        out_specs=pl.BlockSpec((tm, tn), lambda i, j: (i, j)),
        out_shape=jax.ShapeDtypeStruct((t, n), BF16),
        scratch_shapes=[pltpu.VMEM((tm, d), BF16)],
        compiler_params=_params(("parallel", "arbitrary")),
        name="in_proj",
    )(x, g, w)


def _attn_kernel(lamp_ref, q_ref, k0_ref, v0_ref, kn_ref, vn_ref, c_ref, s_ref,
                 qg_ref, kg_ref, sg_ref, o_ref, kbuf, vbuf, *, lam_init):
    i = pl.program_id(0)
    slot = i % 2
    lane = lax.broadcasted_iota(jnp.int32, (1, LANES), 1)
    lo = lane < HEAD_DIM
    first = (lane % HEAD_DIM) < (HEAD_DIM // 2)

    def norm_rope(x, g):
        x2 = x * x
        ss_lo = jnp.sum(jnp.where(lo, x2, 0.0), axis=-1, keepdims=True)
        ss_hi = jnp.sum(jnp.where(lo, 0.0, x2), axis=-1, keepdims=True)
        ms = jnp.where(lo, ss_lo, ss_hi) * (1.0 / HEAD_DIM)
        y = x * lax.rsqrt(ms + EPS) * g
        partner = jnp.where(first, pltpu.roll(y, LANES - HEAD_DIM // 2, 1),
                            pltpu.roll(y, HEAD_DIM // 2, 1))
        return y * c_ref[...] + partner * s_ref[...]

    def prep(k_ref, v_ref, sl):
        kbuf[sl] = norm_rope(k_ref[...].astype(F32), kg_ref[...]).astype(BF16)
        vbuf[sl, :, :V_DIM] = v_ref[...]

    @pl.when(i == 0)
    def _():
        vbuf[:, :, V_DIM:] = jnp.ones((2,) + v0_ref.shape, BF16)
        prep(k0_ref, v0_ref, 0)

    lp = lamp_ref[...]
    a1 = jnp.sum(lp[0:1] * lp[1:2], axis=-1, keepdims=True)
    a2 = jnp.sum(lp[2:3] * lp[3:4], axis=-1, keepdims=True)
    lam = jnp.exp(a1) - jnp.exp(a2) + lam_init

    q = norm_rope(q_ref[...].astype(F32), qg_ref[...]) * (SCORE_SCALE * LOG2E)
    q0 = jnp.where(lo, q, 0.0).astype(BF16)
    q1 = jnp.where(lo, 0.0, q).astype(BF16)
    k = kbuf[slot]
    v = vbuf[slot]
    nt = (((1,), (1,)), ((), ()))
    nsub = q_ref.shape[0] // ATTN_SUB

    def scores(sub):
        rows = slice(sub * ATTN_SUB, (sub + 1) * ATTN_SUB)
        return (lax.dot_general(q0[rows], k, nt, preferred_element_type=F32),
                lax.dot_general(q1[rows], k, nt, preferred_element_type=F32))

    ahead = scores(0)
    for sub in range(nsub):
        rows = slice(sub * ATTN_SUB, (sub + 1) * ATTN_SUB)
        s0, s1 = ahead
        if sub + 1 < nsub:
            ahead = scores(sub + 1)
        e0 = jnp.exp2(s0 - jnp.max(s0, axis=-1, keepdims=True)).astype(BF16)
        e1 = jnp.exp2(s1 - jnp.max(s1, axis=-1, keepdims=True)).astype(BF16)
        ov0 = jnp.dot(e0, v, preferred_element_type=F32)
        ov1 = jnp.dot(e1, v, preferred_element_type=F32)
        r0 = 1.0 / ov0[:, V_DIM:V_DIM + 1]
        r1 = lam / ov1[:, V_DIM:V_DIM + 1]
        o = ov0[:, :V_DIM] * r0 - ov1[:, :V_DIM] * r1
        ms = jnp.mean(o * o, axis=-1, keepdims=True)
        o_ref[rows, :] = (o * lax.rsqrt(ms + EPS) * sg_ref[...] * (1.0 - lam_init)).astype(o_ref.dtype)

    prep(kn_ref, vn_ref, 1 - slot)


def _attention(proj, lamp, cos_t, sin_t, qg, kg, sg, b, s, lam_init):
    n = b * N_HEADS
    const = lambda i: (0, 0)
    cur = lambda col: (lambda i: (i // N_HEADS, col * N_HEADS + i % N_HEADS))
    nxt = lambda col: (lambda i: (jnp.minimum(i + 1, n - 1) // N_HEADS,
                                  col * N_HEADS + jnp.minimum(i + 1, n - 1) % N_HEADS))
    return pl.pallas_call(
        functools.partial(_attn_kernel, lam_init=lam_init),
        grid=(n,),
        in_specs=[pl.BlockSpec((4, HEAD_DIM), const),
                  pl.BlockSpec((s, LANES), cur(0)),
                  pl.BlockSpec((s, LANES), cur(1)),
                  pl.BlockSpec((s, LANES), cur(2)),
                  pl.BlockSpec((s, LANES), nxt(1)),
                  pl.BlockSpec((s, LANES), nxt(2)),
                  pl.BlockSpec((s, LANES), const),
                  pl.BlockSpec((s, LANES), const),
                  pl.BlockSpec((1, LANES), const),
                  pl.BlockSpec((1, LANES), const),
                  pl.BlockSpec((1, LANES), const)],
        out_specs=pl.BlockSpec((s, LANES), cur(0)),
        out_shape=jax.ShapeDtypeStruct((b * s, ATTN_WIDTH), BF16),
        scratch_shapes=[pltpu.VMEM((2, s, LANES), BF16), pltpu.VMEM((2, s, 2 * V_DIM), BF16)],
        compiler_params=_params(("arbitrary",)),
        name="diff_attn",
    )(lamp, proj, proj, proj, proj, proj, cos_t, sin_t, qg, kg, sg)


def _fourier_kernel(u_ref, cs_ref, cc_ref, g_ref, o_ref, pq, *, s):
    fg = FOURIER_GROUP

    @pl.when(pl.program_id(1) == 0)
    def _():
        for gi in range(N_FOURIER_GROUPS):
            r = jnp.dot(u_ref[:, gi * fg:(gi + 1) * fg], cc_ref[...], preferred_element_type=F32)
            pq[0:s, gi * fg:(gi + 1) * fg] = r[:, :fg].astype(BF16)
            pq[s:2 * s, gi * fg:(gi + 1) * fg] = r[:, fg:].astype(BF16)

    f = jnp.dot(cs_ref[...], pq[...], preferred_element_type=F32)
    for gi in range(N_FOURIER_GROUPS):
        fgi = f[:, gi * fg:(gi + 1) * fg]
        ms = jnp.mean(fgi * fgi, axis=-1, keepdims=True)
        o_ref[:, gi * fg:(gi + 1) * fg] = (
            fgi * lax.rsqrt(ms + EPS) * g_ref[:, gi * fg:(gi + 1) * fg]).astype(o_ref.dtype)


def _fourier(proj, cs, cc, g, b, s, tm):
    n = s // tm
    return pl.pallas_call(
        functools.partial(_fourier_kernel, s=s),
        grid=(b, n),
        in_specs=[pl.BlockSpec((s, FOURIER_WIDTH), lambda bi, i: (bi, 3)),
                  pl.BlockSpec((tm, 2 * s), lambda bi, i: (i, 0)),
                  pl.BlockSpec((FOURIER_GROUP, 2 * FOURIER_GROUP), lambda bi, i: (0, 0)),
                  pl.BlockSpec((1, FOURIER_WIDTH), lambda bi, i: (0, 0))],
        out_specs=pl.BlockSpec((tm, FOURIER_WIDTH), lambda bi, i: (bi * n + i, 0)),
        out_shape=jax.ShapeDtypeStruct((b * s, FOURIER_WIDTH), BF16),
        scratch_shapes=[pltpu.VMEM((2 * s, FOURIER_WIDTH), BF16)],
        compiler_params=_params(("parallel", "arbitrary")),
        name="fourier_mix",
    )(proj, cs, cc, g)


def _out_proj_kernel(x_ref, a_ref, f_ref, w1_ref, w2_ref, o_ref):
    o_ref[...] = (x_ref[...]
                  + jnp.dot(a_ref[...], w1_ref[...], preferred_element_type=F32)
                  + jnp.dot(f_ref[...], w2_ref[...], preferred_element_type=F32))


def _out_proj(x, a, f, w, layer, tm, tn):
    t, d = x.shape
    return pl.pallas_call(
        _out_proj_kernel,
        grid=(t // tm, d // tn),
        in_specs=[pl.BlockSpec((tm, tn), lambda i, j: (i, j)),
                  pl.BlockSpec((tm, ATTN_WIDTH), lambda i, j: (i, 0)),
                  pl.BlockSpec((tm, FOURIER_WIDTH), lambda i, j: (i, 0)),
                  pl.BlockSpec((None, ATTN_WIDTH, tn), lambda i, j: (layer, 0, j)),
                  pl.BlockSpec((None, FOURIER_WIDTH, tn), lambda i, j: (layer, 1, j))],
        out_specs=pl.BlockSpec((tm, tn), lambda i, j: (i, j)),
        out_shape=jax.ShapeDtypeStruct((t, d), F32),
        compiler_params=_params(("parallel", "arbitrary")),
        name="out_proj",
    )(x, a, f, w, w)


def _swiglu_partial(xn, wg_ref, wu_ref, wd_ref):
    gate = jnp.dot(xn, wg_ref[...], preferred_element_type=F32)
    up = jnp.dot(xn, wu_ref[...], preferred_element_type=F32)
    h = (gate * (1.0 / (1.0 + jnp.exp(-gate))) * up).astype(BF16)
    return jnp.dot(h, wd_ref[...], preferred_element_type=F32)


def _ffn_dense_kernel(x_ref, g_ref, wg_ref, wu_ref, wd_ref, o_ref, xn_ref):
    @pl.when(pl.program_id(1) == 0)
    def _():
        x = x_ref[...]
        ms = jnp.mean(x * x, axis=-1, keepdims=True)
        xn_ref[...] = (x * lax.rsqrt(ms + EPS) * g_ref[...]).astype(BF16)
        o_ref[...] = x

    o_ref[...] += _swiglu_partial(xn_ref[...], wg_ref, wu_ref, wd_ref)


def _ffn_dense(x, g, wg, wu, wd, layer, tm, tf):
    t, d = x.shape
    ff = wg.shape[2]
    return pl.pallas_call(
        _ffn_dense_kernel,
        grid=(t // tm, ff // tf),
        in_specs=[pl.BlockSpec((tm, d), lambda i, f: (i, 0)),
                  pl.BlockSpec((1, d), lambda i, f: (0, 0)),
                  pl.BlockSpec((None, d, tf), lambda i, f: (layer, 0, f)),
                  pl.BlockSpec((None, d, tf), lambda i, f: (layer, 0, f)),
                  pl.BlockSpec((None, tf, d), lambda i, f: (layer, f, 0))],
        out_specs=pl.BlockSpec((tm, d), lambda i, f: (i, 0)),
        out_shape=jax.ShapeDtypeStruct((t, d), F32),
        scratch_shapes=[pltpu.VMEM((tm, d), BF16)],
        compiler_params=_params(("parallel", "arbitrary")),
        name="swiglu_res",
    )(x, g, wg, wu, wd)


def _ffn_grouped_kernel(eid_ref, nv_ref, x_ref, wg_ref, wu_ref, wd_ref, o_ref):
    i = pl.program_id(0)
    f = pl.program_id(1)

    @pl.when(f == 0)
    def _():
        o_ref[...] = jnp.zeros_like(o_ref)

    @pl.when(i < nv_ref[0])
    def _():
        o_ref[...] += _swiglu_partial(x_ref[...], wg_ref, wu_ref, wd_ref)


def _ffn_grouped(xs, wg, wu, wd, layer, eid, nvalid, tm, tf):
    r, d = xs.shape
    nf = wg.shape[3] // tf
    last = nf - 1

    def tile(i, nv_ref):
        return jnp.minimum(i, nv_ref[0] - 1)

    def fsel(i, f, nv_ref):
        return jnp.where(i < nv_ref[0], f, last)

    def up_map(i, f, eid_ref, nv_ref):
        return (layer, eid_ref[tile(i, nv_ref)], 0, fsel(i, f, nv_ref))

    def down_map(i, f, eid_ref, nv_ref):
        return (layer, eid_ref[tile(i, nv_ref)], fsel(i, f, nv_ref), 0)

    grid_spec = pltpu.PrefetchScalarGridSpec(
        num_scalar_prefetch=2,
        grid=(r // tm, nf),
        in_specs=[pl.BlockSpec((tm, d), lambda i, f, e, n: (tile(i, n), 0)),
                  pl.BlockSpec((None, None, d, tf), up_map),
                  pl.BlockSpec((None, None, d, tf), up_map),
                  pl.BlockSpec((None, None, tf, d), down_map)],
        out_specs=pl.BlockSpec((tm, d), lambda i, f, e, n: (i, 0)),
    )
    return pl.pallas_call(
        _ffn_grouped_kernel,
        grid_spec=grid_spec,
        out_shape=jax.ShapeDtypeStruct((r, d), F32),
        compiler_params=_params(("arbitrary", "arbitrary")),
        name="swiglu_grouped",
    )(eid, nvalid, xs, wg, wu, wd)


def _router_kernel(x_ref, g_ref, wh_ref, wl_ref, ri_ref, rf_ref, cnt_ref, carry, *, tm):
    @pl.when(pl.program_id(0) == 0)
    def _():
        carry[...] = jnp.zeros_like(carry)

    x = x_ref[...]
    ms = jnp.mean(x * x, axis=-1, keepdims=True)
    xn = x * lax.rsqrt(ms + EPS) * g_ref[...]
    xh = xn.astype(BF16)
    xl = (xn - xh.astype(F32)).astype(BF16)
    wh = wh_ref[...]
    logits = (jnp.dot(xh, wh, preferred_element_type=F32)
              + jnp.dot(xl, wh, preferred_element_type=F32)
              + jnp.dot(xh, wl_ref[...], preferred_element_type=F32))

    lane = lax.broadcasted_iota(jnp.int32, (tm, LANES), 1).astype(F32)
    neg = jnp.float32(-jnp.inf)
    lg = jnp.where(lane < N_EXPERTS, logits, neg)
    m1 = jnp.max(lg, axis=-1, keepdims=True)
    i1 = jnp.min(jnp.where(lg == m1, lane, float(LANES)), axis=-1, keepdims=True)
    lg2 = jnp.where(lane == i1, neg, lg)
    m2 = jnp.max(lg2, axis=-1, keepdims=True)
    i2 = jnp.min(jnp.where(lg2 == m2, lane, float(LANES)), axis=-1, keepdims=True)
    e = jnp.exp(m2 - m1)
    g1 = 1.0 / (1.0 + e)
    g2 = e / (1.0 + e)

    sel1 = lane == i1
    sel2 = lane == i2
    onehot = jnp.where(sel1 | sel2, 1.0, 0.0)
    r_i = lax.broadcasted_iota(jnp.int32, (tm, tm), 0)
    c_i = lax.broadcasted_iota(jnp.int32, (tm, tm), 1)
    tri = jnp.where(r_i > c_i, 1.0, 0.0).astype(BF16)
    before = jnp.dot(tri, onehot.astype(BF16), preferred_element_type=F32) + carry[...]
    rank1 = jnp.sum(jnp.where(sel1, before, 0.0), axis=-1, keepdims=True)
    rank2 = jnp.sum(jnp.where(sel2, before, 0.0), axis=-1, keepdims=True)
    carry[...] = carry[...] + jnp.sum(onehot, axis=0, keepdims=True)
    cnt_ref[...] = carry[...]

    ri = jnp.where(lane == 0.0, i1, jnp.where(lane == 1.0, i2,
         jnp.where(lane == 2.0, rank1, jnp.where(lane == 3.0, rank2, 0.0))))
    ri_ref[...] = ri.astype(jnp.int32)
    rf_ref[...] = jnp.where(lane == 0.0, g1, jnp.where(lane == 1.0, g2, 0.0))


def _router(x, g, wh, wl, tm):
    t, d = x.shape
    return pl.pallas_call(
        functools.partial(_router_kernel, tm=tm),
        grid=(t // tm,),
        in_specs=[pl.BlockSpec((tm, d), lambda i: (i, 0)),
                  pl.BlockSpec((1, d), lambda i: (0, 0)),
                  pl.BlockSpec((d, LANES), lambda i: (0, 0)),
                  pl.BlockSpec((d, LANES), lambda i: (0, 0))],
        out_specs=[pl.BlockSpec((tm, LANES), lambda i: (i, 0)),
                   pl.BlockSpec((tm, LANES), lambda i: (i, 0)),
                   pl.BlockSpec((1, LANES), lambda i: (0, 0))],
        out_shape=[jax.ShapeDtypeStruct((t, LANES), jnp.int32),
                   jax.ShapeDtypeStruct((t, LANES), F32),
                   jax.ShapeDtypeStruct((1, LANES), F32)],
        scratch_shapes=[pltpu.VMEM((1, LANES), F32)],
        compiler_params=_params(("arbitrary",)),
        name="router",
    )(x, g, wh, wl)


def _dispatch_kernel(src_ref, nxt_ref, g_ref, x_hbm, o_ref, buf, sem, *, chunk):
    i = pl.program_id(0)
    slot = i % 2

    def gather(idx_ref, s):
        def issue(j, c):
            pltpu.make_async_copy(x_hbm.at[pl.ds(idx_ref[j], 1)], buf.at[s, pl.ds(j, 1)], sem.at[s]).start()
            return c

        lax.fori_loop(0, chunk, issue, 0, unroll=8)

    @pl.when(i == 0)
    def _():
        gather(src_ref, slot)

    @pl.when(i + 1 < pl.num_programs(0))
    def _():
        gather(nxt_ref, 1 - slot)

    pltpu.make_async_copy(x_hbm.at[pl.ds(0, chunk)], buf.at[slot], sem.at[slot]).wait()
    x = buf[slot]
    ms = jnp.mean(x * x, axis=-1, keepdims=True)
    o_ref[...] = (x * lax.rsqrt(ms + EPS) * g_ref[...]).astype(o_ref.dtype)


def _dispatch(x, g, src, chunk):
    r = src.shape[0]
    d = x.shape[1]
    n = r // chunk
    return pl.pallas_call(
        functools.partial(_dispatch_kernel, chunk=chunk),
        grid=(n,),
        in_specs=[pl.BlockSpec((chunk,), lambda i: (i,), memory_space=pltpu.SMEM),
                  pl.BlockSpec((chunk,), lambda i: (jnp.minimum(i + 1, n - 1),), memory_space=pltpu.SMEM),
                  pl.BlockSpec((1, d), lambda i: (0, 0)),
                  pl.BlockSpec(memory_space=pl.ANY)],
        out_specs=pl.BlockSpec((chunk, d), lambda i: (i, 0)),
        out_shape=jax.ShapeDtypeStruct((r, d), BF16),
        scratch_shapes=[pltpu.VMEM((2, chunk, d), F32), pltpu.SemaphoreType.DMA((2,))],
        compiler_params=_params(("arbitrary",)),
        name="moe_dispatch",
    )(src, src, g, x)


def _combine_kernel(pos_ref, nxt_ref, x_ref, rf_ref, y_hbm, *rest, tc, split_tile):
    outs, (buf, sem) = rest[:-2], rest[-2:]
    i = pl.program_id(0)
    slot = i % 2

    def gather(idx_ref, s):
        def issue(t, c):
            for k in range(TOP_K):
                pltpu.make_async_copy(y_hbm.at[pl.ds(idx_ref[TOP_K * t + k], 1)],
                                      buf.at[s, k, pl.ds(t, 1)], sem.at[s]).start()
            return c

        lax.fori_loop(0, tc, issue, 0, unroll=4)

    @pl.when(i == 0)
    def _():
        gather(pos_ref, slot)

    @pl.when(i + 1 < pl.num_programs(0))
    def _():
        gather(nxt_ref, 1 - slot)

    for k in range(TOP_K):
        pltpu.make_async_copy(y_hbm.at[pl.ds(0, tc)], buf.at[slot, k], sem.at[slot]).wait()

    rf = rf_ref[...]
    res = x_ref[...] + rf[:, 0:1] * buf[slot, 0] + rf[:, 1:2] * buf[slot, 1]
    if split_tile is None:
        outs[0][...] = res
    else:
        @pl.when(pl.program_id(0) < split_tile)
        def _():
            outs[0][...] = res

        @pl.when(pl.program_id(0) >= split_tile)
        def _():
            outs[1][...] = res


def _combine(x, rf, y, pos_flat, tc, split_rows=None):
    t, d = x.shape
    if split_rows is None:
        split_tile = None
        out_specs = pl.BlockSpec((tc, d), lambda i: (i, 0))
        out_shape = jax.ShapeDtypeStruct((t, d), F32)
    else:
        split_tile = split_rows // tc
        out_specs = [pl.BlockSpec((tc, d), lambda i: (jnp.minimum(i, split_tile - 1), 0)),
                     pl.BlockSpec((tc, d), lambda i: (jnp.maximum(i - split_tile, 0), 0))]
        out_shape = [jax.ShapeDtypeStruct((split_rows, d), F32), jax.ShapeDtypeStruct((t - split_rows, d), F32)]
    n = t // tc
    return pl.pallas_call(
        functools.partial(_combine_kernel, tc=tc, split_tile=split_tile),
        grid=(n,),
        in_specs=[pl.BlockSpec((TOP_K * tc,), lambda i: (i,), memory_space=pltpu.SMEM),
                  pl.BlockSpec((TOP_K * tc,), lambda i: (jnp.minimum(i + 1, n - 1),), memory_space=pltpu.SMEM),
                  pl.BlockSpec((tc, d), lambda i: (i, 0)),
                  pl.BlockSpec((tc, LANES), lambda i: (i, 0)),
                  pl.BlockSpec(memory_space=pl.ANY)],
        out_specs=out_specs,
        out_shape=out_shape,
        scratch_shapes=[pltpu.VMEM((2, TOP_K, tc, d), F32), pltpu.SemaphoreType.DMA((2,))],
        compiler_params=_params(("arbitrary",)),
        name="moe_combine",
    )(pos_flat, pos_flat, x, rf, y)


def _rope_tables(s):
    pos = jnp.arange(s, dtype=F32)
    inv = ROPE_THETA ** (-jnp.arange(0, HEAD_DIM, 2, dtype=F32) / HEAD_DIM)
    ang = pos[:, None] * inv[None, :]
    cos, sin = jnp.cos(ang), jnp.sin(ang)
    cos_t = jnp.tile(cos, (1, LANES // (HEAD_DIM // 2)))
    sin_t = jnp.tile(jnp.concatenate([-sin, sin], axis=1), (1, LANES // HEAD_DIM))
    return cos_t, sin_t


def _dft_tables(n):
    k = jnp.arange(n, dtype=jnp.int32)
    ang = ((k[:, None] * k[None, :]) % n).astype(F32) * (2.0 * math.pi / n)
    scale = 1.0 / math.sqrt(n)
    return jnp.cos(ang) * scale, jnp.sin(ang) * scale


def _tiles(t, s):
    return dict(in_tm=min(1024, t), in_tn=1024, four_tm=min(512, s),
                out_tm=min(1024, t), out_tn=1024, ffn_tm=min(1024, t), ffn_tf=512,
                moe_tm=min(512, t), moe_tf=1024, router_tm=min(512, t),
                gather_chunk=min(512, t), combine_tc=min(256, t))


def _moe_layer(x, g, w_router, wg, wu, wd, layer, cfg, split_rows=None):
    t, d = x.shape
    tm = cfg["moe_tm"]
    wr = jnp.pad(w_router, ((0, 0), (0, LANES - N_EXPERTS)))
    wr_hi = wr.astype(BF16)
    wr_lo = (wr - wr_hi.astype(F32)).astype(BF16)
    ri, rf, cnt = _router(x, g, wr_hi, wr_lo, cfg["router_tm"])

    n_tiles = (TOP_K * t) // tm + N_EXPERTS
    counts = cnt[0, :N_EXPERTS].astype(jnp.int32)
    padded = ((counts + tm - 1) // tm) * tm
    ends = jnp.cumsum(padded)
    offs = ends - padded
    pos = offs[ri[:, 0:TOP_K]] + ri[:, TOP_K:2 * TOP_K]
    pos_flat = pos.reshape(-1)
    tok = jnp.repeat(jnp.arange(t, dtype=jnp.int32), TOP_K)
    src = jnp.zeros((n_tiles * tm,), jnp.int32).at[pos_flat].set(tok)
    tile_start = jnp.arange(n_tiles, dtype=jnp.int32) * tm
    eid = jnp.minimum(jnp.sum(tile_start[:, None] >= ends[None, :], axis=1), N_EXPERTS - 1).astype(jnp.int32)
    nvalid = (ends[-1:] // tm).astype(jnp.int32)

    xs = _dispatch(x, g, src, cfg["gather_chunk"])
    ys = _ffn_grouped(xs, wg, wu, wd, layer, eid, nvalid, tm, cfg["moe_tf"])
    return _combine(x, rf, ys, pos_flat, cfg["combine_tc"], split_rows)


def _trunk(x3, split_batch, attn_norm_g, w_in, q_norm_g, k_norm_g, lambda_q1, lambda_k1, lambda_q2, lambda_k2,
           subln_g, fourier_norm_g, w_out, ffn_norm_g, dense_w_gate, dense_w_up, dense_w_down,
           router_w, moe_w_gate, moe_w_up, moe_w_down):
    b, s, d = x3.shape
    t = b * s
    depth = w_in.shape[0]
    cfg = _tiles(t, s)
    x = x3.reshape(t, d)

    cos_t, sin_t = _rope_tables(s)
    cs_c, cs_s = _dft_tables(s)
    cs = jnp.concatenate([cs_c, -cs_s], axis=1).astype(BF16)
    cc_c, cc_s = _dft_tables(FOURIER_GROUP)
    cc = jnp.concatenate([cc_c, cc_s], axis=1).astype(BF16)

    rep = LANES // HEAD_DIM
    split_rows = split_batch * s
    w_in, w_out = w_in.astype(BF16), w_out.astype(BF16)
    dense_w = [w.astype(BF16) for w in (dense_w_gate, dense_w_up, dense_w_down)]
    moe_w = [w.astype(BF16) for w in (moe_w_gate, moe_w_up, moe_w_down)]

    for l in range(depth):
        lam_init = 0.8 - 0.6 * math.exp(-0.3 * l)
        proj = _in_proj(x, attn_norm_g[l][None, :], w_in, l, cfg["in_tm"], cfg["in_tn"])
        lamp = jnp.stack([lambda_q1[l], lambda_k1[l], lambda_q2[l], lambda_k2[l]])
        a = _attention(proj, lamp, cos_t, sin_t, jnp.tile(q_norm_g[l], rep)[None, :],
                       jnp.tile(k_norm_g[l], rep)[None, :], subln_g[l][None, :], b, s, lam_init)
        f = _fourier(proj, cs, cc, fourier_norm_g[l][None, :], b, s, cfg["four_tm"])
        x = _out_proj(x, a, f, w_out, l, cfg["out_tm"], cfg["out_tn"])
        gf = ffn_norm_g[l][None, :]
        i = l // 2
        if l % 2 == 0:
            x = _ffn_dense(x, gf, *dense_w, i, cfg["ffn_tm"], cfg["ffn_tf"])
        else:
            last = l == depth - 1
            x = _moe_layer(x, gf, router_w[i], *moe_w, i, cfg, split_rows if last else None)
    if isinstance(x, (list, tuple)):
        y0, y1 = x
    else:
        y0, y1 = x[:split_rows], x[split_rows:]
    return y0.reshape(split_batch, s, d), y1.reshape(b - split_batch, s, d)


def kernel(x_prompt, x_sample, attn_norm_g, w_in, q_norm_g, k_norm_g, lambda_q1, lambda_k1, lambda_q2, lambda_k2, subln_g, fourier_norm_g, w_out, ffn_norm_g, dense_w_gate, dense_w_up, dense_w_down, router_w, moe_w_gate, moe_w_up, moe_w_down):
    nb = x_prompt.shape[0]
    x = jnp.concatenate([x_prompt, x_sample], axis=0)
    return _trunk(x, nb, attn_norm_g, w_in, q_norm_g, k_norm_g, lambda_q1, lambda_k1, lambda_q2, lambda_k2,
                  subln_g, fourier_norm_g, w_out, ffn_norm_g, dense_w_gate, dense_w_up, dense_w_down,
                  router_w, moe_w_gate, moe_w_up, moe_w_down)
```

```python
import functools
import math

import jax
import jax.numpy as jnp
from jax import lax
from jax.experimental import pallas as pl
from jax.experimental.pallas import tpu as pltpu

D_MODEL = 2048
ATTN_WIDTH = 1024
FOURIER_WIDTH = 1024
N_HEADS = 8
HEAD_DIM = 64
V_DIM = 128
N_FOURIER_GROUPS = 4
FOURIER_GROUP = 256
IN_WIDTH = 4096
N_EXPERTS = 8
TOP_K = 2
ROPE_THETA = 10000.0
EPS = 1e-6
SCORE_SCALE = HEAD_DIM ** -0.5
LOG2E = math.log2(math.e)
ATTN_SUB = 256

LANES = 128
VMEM_LIMIT = 56 * 1024 * 1024

F32 = jnp.float32
BF16 = jnp.bfloat16


def _params(sem, vmem=VMEM_LIMIT):
    return pltpu.CompilerParams(dimension_semantics=sem, vmem_limit_bytes=vmem)


def _in_proj_kernel(x_ref, g_ref, w_ref, o_ref, xn_ref):
    @pl.when(pl.program_id(1) == 0)
    def _():
        x = x_ref[...]
        ms = jnp.mean(x * x, axis=-1, keepdims=True)
        xn_ref[...] = (x * lax.rsqrt(ms + EPS) * g_ref[...]).astype(BF16)

    o_ref[...] = jnp.dot(xn_ref[...], w_ref[...], preferred_element_type=F32).astype(o_ref.dtype)


def _in_proj(x, g, w, layer, tm, tn):
    t, d = x.shape
    n = w.shape[2]
    return pl.pallas_call(
        _in_proj_kernel,
        grid=(t // tm, n // tn),
        in_specs=[pl.BlockSpec((tm, d), lambda i, j: (i, 0)),
                  pl.BlockSpec((1, d), lambda i, j: (0, 0)),
                  pl.BlockSpec((None, d, tn), lambda i, j: (layer, 0, j))],
        out_specs=pl.BlockSpec((tm, tn), lambda i, j: (i, j)),
        out_shape=jax.ShapeDtypeStruct((t, n), BF16),
        scratch_shapes=[pltpu.VMEM((tm, d), BF16)],
        compiler_params=_params(("parallel", "arbitrary")),
        name="in_proj",
    )(x, g, w)


def _attn_kernel(lamp_ref, q_ref, k0_ref, v0_ref, kn_ref, vn_ref, c_ref, s_ref,
                 qg_ref, kg_ref, sg_ref, o_ref, kbuf, vbuf, *, lam_init):
    i = pl.program_id(0)
    slot = i % 2
    lane = lax.broadcasted_iota(jnp.int32, (1, LANES), 1)
    lo = lane < HEAD_DIM
    first = (lane % HEAD_DIM) < (HEAD_DIM // 2)

    def norm_rope(x, g):
        x2 = x * x
        ss_lo = jnp.sum(jnp.where(lo, x2, 0.0), axis=-1, keepdims=True)
        ss_hi = jnp.sum(jnp.where(lo, 0.0, x2), axis=-1, keepdims=True)
        ms = jnp.where(lo, ss_lo, ss_hi) * (1.0 / HEAD_DIM)
        y = x * lax.rsqrt(ms + EPS) * g
        partner = jnp.where(first, pltpu.roll(y, LANES - HEAD_DIM // 2, 1),
                            pltpu.roll(y, HEAD_DIM // 2, 1))
        return y * c_ref[...] + partner * s_ref[...]

    def prep(k_ref, v_ref, sl):
        kbuf[sl] = norm_rope(k_ref[...].astype(F32), kg_ref[...]).astype(BF16)
        vbuf[sl, :, :V_DIM] = v_ref[...]

    @pl.when(i == 0)
    def _():
        vbuf[:, :, V_DIM:] = jnp.ones((2,) + v0_ref.shape, BF16)
        prep(k0_ref, v0_ref, 0)

    lp = lamp_ref[...]
    a1 = jnp.sum(lp[0:1] * lp[1:2], axis=-1, keepdims=True)
    a2 = jnp.sum(lp[2:3] * lp[3:4], axis=-1, keepdims=True)
    lam = jnp.exp(a1) - jnp.exp(a2) + lam_init

    q = norm_rope(q_ref[...].astype(F32), qg_ref[...]) * (SCORE_SCALE * LOG2E)
    q0 = jnp.where(lo, q, 0.0).astype(BF16)
    q1 = jnp.where(lo, 0.0, q).astype(BF16)
    k = kbuf[slot]
    v = vbuf[slot]
    nt = (((1,), (1,)), ((), ()))
    nsub = q_ref.shape[0] // ATTN_SUB

    def scores(sub):
        rows = slice(sub * ATTN_SUB, (sub + 1) * ATTN_SUB)
        return (lax.dot_general(q0[rows], k, nt, preferred_element_type=F32),
                lax.dot_general(q1[rows], k, nt, preferred_element_type=F32))

    ahead = scores(0)
    for sub in range(nsub):
        rows = slice(sub * ATTN_SUB, (sub + 1) * ATTN_SUB)
        s0, s1 = ahead
        if sub + 1 < nsub:
            ahead = scores(sub + 1)
        e0 = jnp.exp2(s0 - jnp.max(s0, axis=-1, keepdims=True)).astype(BF16)
        e1 = jnp.exp2(s1 - jnp.max(s1, axis=-1, keepdims=True)).astype(BF16)
        ov0 = jnp.dot(e0, v, preferred_element_type=F32)
        ov1 = jnp.dot(e1, v, preferred_element_type=F32)
        r0 = 1.0 / ov0[:, V_DIM:V_DIM + 1]
        r1 = lam / ov1[:, V_DIM:V_DIM + 1]
        o = ov0[:, :V_DIM] * r0 - ov1[:, :V_DIM] * r1
        ms = jnp.mean(o * o, axis=-1, keepdims=True)
        o_ref[rows, :] = (o * lax.rsqrt(ms + EPS) * sg_ref[...] * (1.0 - lam_init)).astype(o_ref.dtype)

    prep(kn_ref, vn_ref, 1 - slot)


def _attention(proj, lamp, cos_t, sin_t, qg, kg, sg, b, s, lam_init):
    n = b * N_HEADS
    const = lambda i: (0, 0)
    cur = lambda col: (lambda i: (i // N_HEADS, col * N_HEADS + i % N_HEADS))
    nxt = lambda col: (lambda i: (jnp.minimum(i + 1, n - 1) // N_HEADS,
                                  col * N_HEADS + jnp.minimum(i + 1, n - 1) % N_HEADS))
    return pl.pallas_call(
        functools.partial(_attn_kernel, lam_init=lam_init),
        grid=(n,),
        in_specs=[pl.BlockSpec((4, HEAD_DIM), const),
                  pl.BlockSpec((s, LANES), cur(0)),
                  pl.BlockSpec((s, LANES), cur(1)),
                  pl.BlockSpec((s, LANES), cur(2)),
                  pl.BlockSpec((s, LANES), nxt(1)),
                  pl.BlockSpec((s, LANES), nxt(2)),
                  pl.BlockSpec((s, LANES), const),
                  pl.BlockSpec((s, LANES), const),
                  pl.BlockSpec((1, LANES), const),
                  pl.BlockSpec((1, LANES), const),
                  pl.BlockSpec((1, LANES), const)],
        out_specs=pl.BlockSpec((s, LANES), cur(0)),
        out_shape=jax.ShapeDtypeStruct((b * s, ATTN_WIDTH), BF16),
        scratch_shapes=[pltpu.VMEM((2, s, LANES), BF16), pltpu.VMEM((2, s, 2 * V_DIM), BF16)],
        compiler_params=_params(("arbitrary",)),
        name="diff_attn",
    )(lamp, proj, proj, proj, proj, proj, cos_t, sin_t, qg, kg, sg)


def _fourier_kernel(u_ref, cs_ref, cc_ref, g_ref, o_ref, pq, *, s):
    fg = FOURIER_GROUP

    @pl.when(pl.program_id(1) == 0)
    def _():
        for gi in range(N_FOURIER_GROUPS):
            r = jnp.dot(u_ref[:, gi * fg:(gi + 1) * fg], cc_ref[...], preferred_element_type=F32)
            pq[0:s, gi * fg:(gi + 1) * fg] = r[:, :fg].astype(BF16)
            pq[s:2 * s, gi * fg:(gi + 1) * fg] = r[:, fg:].astype(BF16)

    f = jnp.dot(cs_ref[...], pq[...], preferred_element_type=F32)
    for gi in range(N_FOURIER_GROUPS):
        fgi = f[:, gi * fg:(gi + 1) * fg]
        ms = jnp.mean(fgi * fgi, axis=-1, keepdims=True)
        o_ref[:, gi * fg:(gi + 1) * fg] = (
            fgi * lax.rsqrt(ms + EPS) * g_ref[:, gi * fg:(gi + 1) * fg]).astype(o_ref.dtype)


def _fourier(proj, cs, cc, g, b, s, tm):
    n = s // tm
    return pl.pallas_call(
        functools.partial(_fourier_kernel, s=s),
        grid=(b, n),
        in_specs=[pl.BlockSpec((s, FOURIER_WIDTH), lambda bi, i: (bi, 3)),
                  pl.BlockSpec((tm, 2 * s), lambda bi, i: (i, 0)),
                  pl.BlockSpec((FOURIER_GROUP, 2 * FOURIER_GROUP), lambda bi, i: (0, 0)),
                  pl.BlockSpec((1, FOURIER_WIDTH), lambda bi, i: (0, 0))],
        out_specs=pl.BlockSpec((tm, FOURIER_WIDTH), lambda bi, i: (bi * n + i, 0)),
        out_shape=jax.ShapeDtypeStruct((b * s, FOURIER_WIDTH), BF16),
        scratch_shapes=[pltpu.VMEM((2 * s, FOURIER_WIDTH), BF16)],
        compiler_params=_params(("parallel", "arbitrary")),
        name="fourier_mix",
    )(proj, cs, cc, g)


def _out_proj_kernel(x_ref, a_ref, f_ref, w1_ref, w2_ref, o_ref):
    o_ref[...] = (x_ref[...]
                  + jnp.dot(a_ref[...], w1_ref[...], preferred_element_type=F32)
                  + jnp.dot(f_ref[...], w2_ref[...], preferred_element_type=F32))


def _out_proj(x, a, f, w, layer, tm, tn):
    t, d = x.shape
    return pl.pallas_call(
        _out_proj_kernel,
        grid=(t // tm, d // tn),
        in_specs=[pl.BlockSpec((tm, tn), lambda i, j: (i, j)),
                  pl.BlockSpec((tm, ATTN_WIDTH), lambda i, j: (i, 0)),
                  pl.BlockSpec((tm, FOURIER_WIDTH), lambda i, j: (i, 0)),
                  pl.BlockSpec((None, ATTN_WIDTH, tn), lambda i, j: (layer, 0, j)),
                  pl.BlockSpec((None, FOURIER_WIDTH, tn), lambda i, j: (layer, 1, j))],
        out_specs=pl.BlockSpec((tm, tn), lambda i, j: (i, j)),
        out_shape=jax.ShapeDtypeStruct((t, d), F32),
        compiler_params=_params(("parallel", "arbitrary")),
        name="out_proj",
    )(x, a, f, w, w)


def _swiglu_partial(xn, wg_ref, wu_ref, wd_ref):
    gate = jnp.dot(xn, wg_ref[...], preferred_element_type=F32)
    up = jnp.dot(xn, wu_ref[...], preferred_element_type=F32)
    h = (gate * (1.0 / (1.0 + jnp.exp(-gate))) * up).astype(BF16)
    return jnp.dot(h, wd_ref[...], preferred_element_type=F32)


def _ffn_dense_kernel(x_ref, g_ref, wg_ref, wu_ref, wd_ref, *rest, n_cast):
    cast_in, o_ref, cast_out, xn_ref = rest[:n_cast], rest[n_cast], rest[n_cast + 1:-1], rest[-1]

    @pl.when(pl.program_id(1) == 0)
    def _():
        x = x_ref[...]
        ms = jnp.mean(x * x, axis=-1, keepdims=True)
        xn_ref[...] = (x * lax.rsqrt(ms + EPS) * g_ref[...]).astype(BF16)
        o_ref[...] = x

    o_ref[...] += _swiglu_partial(xn_ref[...], wg_ref, wu_ref, wd_ref)
    for src, dst in zip(cast_in, cast_out):
        dst[...] = src[...].astype(BF16)


def _cast_rows(rows, steps):
    for h in range(16, rows + 1, 16):
        if rows % h == 0 and rows // h <= steps:
            return h
    raise ValueError("no slab height fits")


def _ffn_dense(x, g, wg, wu, wd, layer, tm, tf, cast=(), cast_layer=0):
    t, d = x.shape
    ff = wg.shape[2]
    nf = ff // tf
    steps = (t // tm) * nf
    cast_specs, cast_out_specs, cast_shapes = [], [], []
    for w in cast:
        _, rows, cols = w.shape
        h = _cast_rows(rows, steps)
        nblk = rows // h
        slab = lambda i, f, nblk=nblk: jnp.minimum(i * nf + f, nblk - 1)
        cast_specs.append(pl.BlockSpec((None, h, cols), lambda i, f, slab=slab: (cast_layer, slab(i, f), 0)))
        cast_out_specs.append(pl.BlockSpec((h, cols), lambda i, f, slab=slab: (slab(i, f), 0)))
        cast_shapes.append(jax.ShapeDtypeStruct((rows, cols), BF16))
    outs = pl.pallas_call(
        functools.partial(_ffn_dense_kernel, n_cast=len(cast)),
        grid=(t // tm, nf),
        in_specs=[pl.BlockSpec((tm, d), lambda i, f: (i, 0)),
                  pl.BlockSpec((1, d), lambda i, f: (0, 0)),
                  pl.BlockSpec((None, d, tf), lambda i, f: (layer, 0, f)),
                  pl.BlockSpec((None, d, tf), lambda i, f: (layer, 0, f)),
                  pl.BlockSpec((None, tf, d), lambda i, f: (layer, f, 0))] + cast_specs,
        out_specs=[pl.BlockSpec((tm, d), lambda i, f: (i, 0))] + cast_out_specs,
        out_shape=[jax.ShapeDtypeStruct((t, d), F32)] + cast_shapes,
        scratch_shapes=[pltpu.VMEM((tm, d), BF16)],
        compiler_params=_params(("arbitrary", "arbitrary")),
        name="swiglu_res",
    )(x, g, wg, wu, wd, *cast)
    return outs[0], outs[1:]


def _ffn_grouped_kernel(eid_ref, nv_ref, x_ref, wg_ref, wu_ref, wd_ref, o_ref):
    i = pl.program_id(0)
    f = pl.program_id(1)

    @pl.when(f == 0)
    def _():
        o_ref[...] = jnp.zeros_like(o_ref)

    @pl.when(i < nv_ref[0])
    def _():
        o_ref[...] += _swiglu_partial(x_ref[...], wg_ref, wu_ref, wd_ref)


def _ffn_grouped(xs, wg, wu, wd, layer, eid, nvalid, tm, tf):
    r, d = xs.shape
    nf = wg.shape[3] // tf
    last = nf - 1

    def tile(i, nv_ref):
        return jnp.minimum(i, nv_ref[0] - 1)

    def fsel(i, f, nv_ref):
        return jnp.where(i < nv_ref[0], f, last)

    def up_map(i, f, eid_ref, nv_ref):
        return (layer, eid_ref[tile(i, nv_ref)], 0, fsel(i, f, nv_ref))

    def down_map(i, f, eid_ref, nv_ref):
        return (layer, eid_ref[tile(i, nv_ref)], fsel(i, f, nv_ref), 0)

    grid_spec = pltpu.PrefetchScalarGridSpec(
        num_scalar_prefetch=2,
        grid=(r // tm, nf),
        in_specs=[pl.BlockSpec((tm, d), lambda i, f, e, n: (tile(i, n), 0)),
                  pl.BlockSpec((None, None, d, tf), up_map),
                  pl.BlockSpec((None, None, d, tf), up_map),
                  pl.BlockSpec((None, None, tf, d), down_map)],
        out_specs=pl.BlockSpec((tm, d), lambda i, f, e, n: (i, 0)),
    )
    return pl.pallas_call(
        _ffn_grouped_kernel,
        grid_spec=grid_spec,
        out_shape=jax.ShapeDtypeStruct((r, d), F32),
        compiler_params=_params(("arbitrary", "arbitrary")),
        name="swiglu_grouped",
    )(eid, nvalid, xs, wg, wu, wd)


def _router_kernel(x_ref, g_ref, wh_ref, wl_ref, ri_ref, rf_ref, cnt_ref, carry, *, tm):
    @pl.when(pl.program_id(0) == 0)
    def _():
        carry[...] = jnp.zeros_like(carry)

    x = x_ref[...]
    ms = jnp.mean(x * x, axis=-1, keepdims=True)
    xn = x * lax.rsqrt(ms + EPS) * g_ref[...]
    xh = xn.astype(BF16)
    xl = (xn - xh.astype(F32)).astype(BF16)
    wh = wh_ref[...]
    logits = (jnp.dot(xh, wh, preferred_element_type=F32)
              + jnp.dot(xl, wh, preferred_element_type=F32)
              + jnp.dot(xh, wl_ref[...], preferred_element_type=F32))

    lane = lax.broadcasted_iota(jnp.int32, (tm, LANES), 1).astype(F32)
    neg = jnp.float32(-jnp.inf)
    lg = jnp.where(lane < N_EXPERTS, logits, neg)
    m1 = jnp.max(lg, axis=-1, keepdims=True)
    i1 = jnp.min(jnp.where(lg == m1, lane, float(LANES)), axis=-1, keepdims=True)
    lg2 = jnp.where(lane == i1, neg, lg)
    m2 = jnp.max(lg2, axis=-1, keepdims=True)
    i2 = jnp.min(jnp.where(lg2 == m2, lane, float(LANES)), axis=-1, keepdims=True)
    e = jnp.exp(m2 - m1)
    g1 = 1.0 / (1.0 + e)
    g2 = e / (1.0 + e)

    sel1 = lane == i1
    sel2 = lane == i2
    onehot = jnp.where(sel1 | sel2, 1.0, 0.0)
    r_i = lax.broadcasted_iota(jnp.int32, (tm, tm), 0)
    c_i = lax.broadcasted_iota(jnp.int32, (tm, tm), 1)
    tri = jnp.where(r_i > c_i, 1.0, 0.0).astype(BF16)
    before = jnp.dot(tri, onehot.astype(BF16), preferred_element_type=F32) + carry[...]
    rank1 = jnp.sum(jnp.where(sel1, before, 0.0), axis=-1, keepdims=True)
    rank2 = jnp.sum(jnp.where(sel2, before, 0.0), axis=-1, keepdims=True)
    carry[...] = carry[...] + jnp.sum(onehot, axis=0, keepdims=True)
    cnt_ref[...] = carry[...]

    ri = jnp.where(lane == 0.0, i1, jnp.where(lane == 1.0, i2,
         jnp.where(lane == 2.0, rank1, jnp.where(lane == 3.0, rank2, 0.0))))
    ri_ref[...] = ri.astype(jnp.int32)
    rf_ref[...] = jnp.where(lane == 0.0, g1, jnp.where(lane == 1.0, g2, 0.0))


def _router(x, g, wh, wl, tm):
    t, d = x.shape
    return pl.pallas_call(
        functools.partial(_router_kernel, tm=tm),
        grid=(t // tm,),
        in_specs=[pl.BlockSpec((tm, d), lambda i: (i, 0)),
                  pl.BlockSpec((1, d), lambda i: (0, 0)),
                  pl.BlockSpec((d, LANES), lambda i: (0, 0)),
                  pl.BlockSpec((d, LANES), lambda i: (0, 0))],
        out_specs=[pl.BlockSpec((tm, LANES), lambda i: (i, 0)),
                   pl.BlockSpec((tm, LANES), lambda i: (i, 0)),
                   pl.BlockSpec((1, LANES), lambda i: (0, 0))],
        out_shape=[jax.ShapeDtypeStruct((t, LANES), jnp.int32),
                   jax.ShapeDtypeStruct((t, LANES), F32),
                   jax.ShapeDtypeStruct((1, LANES), F32)],
        scratch_shapes=[pltpu.VMEM((1, LANES), F32)],
        compiler_params=_params(("arbitrary",)),
        name="router",
    )(x, g, wh, wl)


def _dispatch_kernel(src_ref, nxt_ref, g_ref, x_hbm, o_ref, buf, sem, *, chunk):
    i = pl.program_id(0)
    slot = i % 2

    def gather(idx_ref, s):
        def issue(j, c):
            pltpu.make_async_copy(x_hbm.at[pl.ds(idx_ref[j], 1)], buf.at[s, pl.ds(j, 1)], sem.at[s]).start()
            return c

        lax.fori_loop(0, chunk, issue, 0, unroll=8)

    @pl.when(i == 0)
    def _():
        gather(src_ref, slot)

    @pl.when(i + 1 < pl.num_programs(0))
    def _():
        gather(nxt_ref, 1 - slot)

    pltpu.make_async_copy(x_hbm.at[pl.ds(0, chunk)], buf.at[slot], sem.at[slot]).wait()
    x = buf[slot]
    ms = jnp.mean(x * x, axis=-1, keepdims=True)
    o_ref[...] = (x * lax.rsqrt(ms + EPS) * g_ref[...]).astype(o_ref.dtype)


def _dispatch(x, g, src, chunk):
    r = src.shape[0]
    d = x.shape[1]
    n = r // chunk
    return pl.pallas_call(
        functools.partial(_dispatch_kernel, chunk=chunk),
        grid=(n,),
        in_specs=[pl.BlockSpec((chunk,), lambda i: (i,), memory_space=pltpu.SMEM),
                  pl.BlockSpec((chunk,), lambda i: (jnp.minimum(i + 1, n - 1),), memory_space=pltpu.SMEM),
                  pl.BlockSpec((1, d), lambda i: (0, 0)),
                  pl.BlockSpec(memory_space=pl.ANY)],
        out_specs=pl.BlockSpec((chunk, d), lambda i: (i, 0)),
        out_shape=jax.ShapeDtypeStruct((r, d), BF16),
        scratch_shapes=[pltpu.VMEM((2, chunk, d), F32), pltpu.SemaphoreType.DMA((2,))],
        compiler_params=_params(("arbitrary",)),
        name="moe_dispatch",
    )(src, src, g, x)


def _combine_kernel(pos_ref, nxt_ref, x_ref, rf_ref, y_hbm, *rest, tc, split_tile):
    outs, (buf, sem) = rest[:-2], rest[-2:]
    i = pl.program_id(0)
    slot = i % 2

    def gather(idx_ref, s):
        def issue(t, c):
            for k in range(TOP_K):
                pltpu.make_async_copy(y_hbm.at[pl.ds(idx_ref[TOP_K * t + k], 1)],
                                      buf.at[s, k, pl.ds(t, 1)], sem.at[s]).start()
            return c

        lax.fori_loop(0, tc, issue, 0, unroll=4)

    @pl.when(i == 0)
    def _():
        gather(pos_ref, slot)

    @pl.when(i + 1 < pl.num_programs(0))
    def _():
        gather(nxt_ref, 1 - slot)

    for k in range(TOP_K):
        pltpu.make_async_copy(y_hbm.at[pl.ds(0, tc)], buf.at[slot, k], sem.at[slot]).wait()

    rf = rf_ref[...]
    res = x_ref[...] + rf[:, 0:1] * buf[slot, 0] + rf[:, 1:2] * buf[slot, 1]
    if split_tile is None:
        outs[0][...] = res
    else:
        @pl.when(pl.program_id(0) < split_tile)
        def _():
            outs[0][...] = res

        @pl.when(pl.program_id(0) >= split_tile)
        def _():
            outs[1][...] = res


def _combine(x, rf, y, pos_flat, tc, split_rows=None):
    t, d = x.shape
    if split_rows is None:
        split_tile = None
        out_specs = pl.BlockSpec((tc, d), lambda i: (i, 0))
        out_shape = jax.ShapeDtypeStruct((t, d), F32)
    else:
        split_tile = split_rows // tc
        out_specs = [pl.BlockSpec((tc, d), lambda i: (jnp.minimum(i, split_tile - 1), 0)),
                     pl.BlockSpec((tc, d), lambda i: (jnp.maximum(i - split_tile, 0), 0))]
        out_shape = [jax.ShapeDtypeStruct((split_rows, d), F32), jax.ShapeDtypeStruct((t - split_rows, d), F32)]
    n = t // tc
    return pl.pallas_call(
        functools.partial(_combine_kernel, tc=tc, split_tile=split_tile),
        grid=(n,),
        in_specs=[pl.BlockSpec((TOP_K * tc,), lambda i: (i,), memory_space=pltpu.SMEM),
                  pl.BlockSpec((TOP_K * tc,), lambda i: (jnp.minimum(i + 1, n - 1),), memory_space=pltpu.SMEM),
                  pl.BlockSpec((tc, d), lambda i: (i, 0)),
                  pl.BlockSpec((tc, LANES), lambda i: (i, 0)),
                  pl.BlockSpec(memory_space=pl.ANY)],
        out_specs=out_specs,
        out_shape=out_shape,
        scratch_shapes=[pltpu.VMEM((2, TOP_K, tc, d), F32), pltpu.SemaphoreType.DMA((2,))],
        compiler_params=_params(("arbitrary",)),
        name="moe_combine",
    )(pos_flat, pos_flat, x, rf, y)


def _rope_tables(s):
    pos = jnp.arange(s, dtype=F32)
    inv = ROPE_THETA ** (-jnp.arange(0, HEAD_DIM, 2, dtype=F32) / HEAD_DIM)
    ang = pos[:, None] * inv[None, :]
    cos, sin = jnp.cos(ang), jnp.sin(ang)
    cos_t = jnp.tile(cos, (1, LANES // (HEAD_DIM // 2)))
    sin_t = jnp.tile(jnp.concatenate([-sin, sin], axis=1), (1, LANES // HEAD_DIM))
    return cos_t, sin_t


def _dft_tables(n):
    k = jnp.arange(n, dtype=jnp.int32)
    ang = ((k[:, None] * k[None, :]) % n).astype(F32) * (2.0 * math.pi / n)
    scale = 1.0 / math.sqrt(n)
    return jnp.cos(ang) * scale, jnp.sin(ang) * scale


def _tiles(t, s):
    return dict(in_tm=min(1024, t), in_tn=1024, four_tm=min(512, s),
                out_tm=min(1024, t), out_tn=1024, ffn_tm=min(512, t), ffn_tf=512,
                moe_tm=min(512, t), moe_tf=1024, router_tm=min(512, t),
                gather_chunk=min(512, t), combine_tc=min(256, t))


def _moe_layer(x, g, w_router, wg, wu, wd, layer, cfg, split_rows=None):
    t, d = x.shape
    tm = cfg["moe_tm"]
    wr = jnp.pad(w_router, ((0, 0), (0, LANES - N_EXPERTS)))
    wr_hi = wr.astype(BF16)
    wr_lo = (wr - wr_hi.astype(F32)).astype(BF16)
    ri, rf, cnt = _router(x, g, wr_hi, wr_lo, cfg["router_tm"])

    n_tiles = (TOP_K * t) // tm + N_EXPERTS
    counts = cnt[0, :N_EXPERTS].astype(jnp.int32)
    padded = ((counts + tm - 1) // tm) * tm
    ends = jnp.cumsum(padded)
    offs = ends - padded
    pos = offs[ri[:, 0:TOP_K]] + ri[:, TOP_K:2 * TOP_K]
    pos_flat = pos.reshape(-1)
    tok = jnp.repeat(jnp.arange(t, dtype=jnp.int32), TOP_K)
    src = jnp.zeros((n_tiles * tm,), jnp.int32).at[pos_flat].set(tok)
    tile_start = jnp.arange(n_tiles, dtype=jnp.int32) * tm
    eid = jnp.minimum(jnp.sum(tile_start[:, None] >= ends[None, :], axis=1), N_EXPERTS - 1).astype(jnp.int32)
    nvalid = (ends[-1:] // tm).astype(jnp.int32)

    xs = _dispatch(x, g, src, cfg["gather_chunk"])
    ys = _ffn_grouped(xs, wg, wu, wd, layer, eid, nvalid, tm, cfg["moe_tf"])
    return _combine(x, rf, ys, pos_flat, cfg["combine_tc"], split_rows)


def _trunk(x3, split_batch, attn_norm_g, w_in, q_norm_g, k_norm_g, lambda_q1, lambda_k1, lambda_q2, lambda_k2,
           subln_g, fourier_norm_g, w_out, ffn_norm_g, dense_w_gate, dense_w_up, dense_w_down,
           router_w, moe_w_gate, moe_w_up, moe_w_down):
    b, s, d = x3.shape
    t = b * s
    depth = w_in.shape[0]
    cfg = _tiles(t, s)
    x = x3.reshape(t, d)

    cos_t, sin_t = _rope_tables(s)
    cs_c, cs_s = _dft_tables(s)
    cs = jnp.concatenate([cs_c, -cs_s], axis=1).astype(BF16)
    cc_c, cc_s = _dft_tables(FOURIER_GROUP)
    cc = jnp.concatenate([cc_c, cc_s], axis=1).astype(BF16)

    rep = LANES // HEAD_DIM
    split_rows = split_batch * s
    w_in, w_out = w_in.astype(BF16), w_out.astype(BF16)
    dense_w = [w.astype(BF16) for w in (dense_w_gate, dense_w_up, dense_w_down)]
    moe_shapes = [w.shape[1:] for w in (moe_w_gate, moe_w_up, moe_w_down)]
    moe_f32 = [w.reshape(w.shape[0], w.shape[1] * w.shape[2], w.shape[3])
               for w in (moe_w_gate, moe_w_up, moe_w_down)]
    moe_w = None

    for l in range(depth):
        lam_init = 0.8 - 0.6 * math.exp(-0.3 * l)
        proj = _in_proj(x, attn_norm_g[l][None, :], w_in, l, cfg["in_tm"], cfg["in_tn"])
        lamp = jnp.stack([lambda_q1[l], lambda_k1[l], lambda_q2[l], lambda_k2[l]])
        a = _attention(proj, lamp, cos_t, sin_t, jnp.tile(q_norm_g[l], rep)[None, :],
                       jnp.tile(k_norm_g[l], rep)[None, :], subln_g[l][None, :], b, s, lam_init)
        f = _fourier(proj, cs, cc, fourier_norm_g[l][None, :], b, s, cfg["four_tm"])
        x = _out_proj(x, a, f, w_out, l, cfg["out_tm"], cfg["out_tn"])
        gf = ffn_norm_g[l][None, :]
        i = l // 2
        if l % 2 == 0:
            has_moe = l + 1 < depth
            x, cast = _ffn_dense(x, gf, *dense_w, i, cfg["ffn_tm"], cfg["ffn_tf"],
                                 cast=moe_f32 if has_moe else (), cast_layer=i)
            moe_w = [w.reshape((1,) + shp) for w, shp in zip(cast, moe_shapes)]
        else:
            last = l == depth - 1
            x = _moe_layer(x, gf, router_w[i], *moe_w, 0, cfg, split_rows if last else None)
    if isinstance(x, (list, tuple)):
        y0, y1 = x
    else:
        y0, y1 = x[:split_rows], x[split_rows:]
    return y0.reshape(split_batch, s, d), y1.reshape(b - split_batch, s, d)


def kernel(x_prompt, x_sample, attn_norm_g, w_in, q_norm_g, k_norm_g, lambda_q1, lambda_k1, lambda_q2, lambda_k2, subln_g, fourier_norm_g, w_out, ffn_norm_g, dense_w_gate, dense_w_up, dense_w_down, router_w, moe_w_gate, moe_w_up, moe_w_down):
    nb = x_prompt.shape[0]
    x = jnp.concatenate([x_prompt, x_sample], axis=0)
    return _trunk(x, nb, attn_norm_g, w_in, q_norm_g, k_norm_g, lambda_q1, lambda_k1, lambda_q2, lambda_k2,
                  subln_g, fourier_norm_g, w_out, ffn_norm_g, dense_w_gate, dense_w_up, dense_w_down,
                  router_w, moe_w_gate, moe_w_up, moe_w_down)
```

```python
import functools
import math

import jax
import jax.numpy as jnp
from jax import lax
from jax.experimental import pallas as pl
from jax.experimental.pallas import tpu as pltpu

D_MODEL = 2048
ATTN_WIDTH = 1024
FOURIER_WIDTH = 1024
N_HEADS = 8
HEAD_DIM = 64
V_DIM = 128
N_FOURIER_GROUPS = 4
FOURIER_GROUP = 256
IN_WIDTH = 4096
N_EXPERTS = 8
TOP_K = 2
ROPE_THETA = 10000.0
EPS = 1e-6
SCORE_SCALE = HEAD_DIM ** -0.5
LOG2E = math.log2(math.e)
ATTN_SUB = 256

LANES = 128
VMEM_LIMIT = 56 * 1024 * 1024

F32 = jnp.float32
BF16 = jnp.bfloat16


def _params(sem, vmem=VMEM_LIMIT):
    return pltpu.CompilerParams(dimension_semantics=sem, vmem_limit_bytes=vmem)


def _in_proj_kernel(x_ref, g_ref, w_ref, o_ref, xn_ref):
    @pl.when(pl.program_id(1) == 0)
    def _():
        x = x_ref[...]
        ms = jnp.mean(x * x, axis=-1, keepdims=True)
        xn_ref[...] = (x * lax.rsqrt(ms + EPS) * g_ref[...]).astype(BF16)

    o_ref[...] = jnp.dot(xn_ref[...], w_ref[...], preferred_element_type=F32).astype(o_ref.dtype)


def _in_proj(x, g, w, layer, tm, tn):
    t, d = x.shape
    n = w.shape[2]
    return pl.pallas_call(
        _in_proj_kernel,
        grid=(t // tm, n // tn),
        in_specs=[pl.BlockSpec((tm, d), lambda i, j: (i, 0)),
                  pl.BlockSpec((1, d), lambda i, j: (0, 0)),
                  pl.BlockSpec((None, d, tn), lambda i, j: (layer, 0, j))],
        out_specs=pl.BlockSpec((tm, tn), lambda i, j: (i, j)),
        out_shape=jax.ShapeDtypeStruct((t, n), BF16),
        scratch_shapes=[pltpu.VMEM((tm, d), BF16)],
        compiler_params=_params(("parallel", "arbitrary")),
        name="in_proj",
    )(x, g, w)


def _attn_kernel(lamp_ref, q_ref, k0_ref, v0_ref, kn_ref, vn_ref, c_ref, s_ref,
                 qg_ref, kg_ref, sg_ref, o_ref, kbuf, vbuf, *, lam_init):
    i = pl.program_id(0)
    slot = i % 2
    lane = lax.broadcasted_iota(jnp.int32, (1, LANES), 1)
    lo = lane < HEAD_DIM
    first = (lane % HEAD_DIM) < (HEAD_DIM // 2)

    def norm_rope(x, g):
        x2 = x * x
        ss_lo = jnp.sum(jnp.where(lo, x2, 0.0), axis=-1, keepdims=True)
        ss_hi = jnp.sum(jnp.where(lo, 0.0, x2), axis=-1, keepdims=True)
        ms = jnp.where(lo, ss_lo, ss_hi) * (1.0 / HEAD_DIM)
        y = x * lax.rsqrt(ms + EPS) * g
        partner = jnp.where(first, pltpu.roll(y, LANES - HEAD_DIM // 2, 1),
                            pltpu.roll(y, HEAD_DIM // 2, 1))
        return y * c_ref[...] + partner * s_ref[...]

    def prep(k_ref, v_ref, sl):
        kbuf[sl] = norm_rope(k_ref[...].astype(F32), kg_ref[...]).astype(BF16)
        vbuf[sl, :, :V_DIM] = v_ref[...]

    @pl.when(i == 0)
    def _():
        vbuf[:, :, V_DIM:] = jnp.ones((2,) + v0_ref.shape, BF16)
        prep(k0_ref, v0_ref, 0)

    lp = lamp_ref[...]
    a1 = jnp.sum(lp[0:1] * lp[1:2], axis=-1, keepdims=True)
    a2 = jnp.sum(lp[2:3] * lp[3:4], axis=-1, keepdims=True)
    lam = jnp.exp(a1) - jnp.exp(a2) + lam_init

    q = norm_rope(q_ref[...].astype(F32), qg_ref[...]) * (SCORE_SCALE * LOG2E)
    q0 = jnp.where(lo, q, 0.0).astype(BF16)
    q1 = jnp.where(lo, 0.0, q).astype(BF16)
    k = kbuf[slot]
    v = vbuf[slot]
    nt = (((1,), (1,)), ((), ()))
    nsub = q_ref.shape[0] // ATTN_SUB

    def scores(sub):
        rows = slice(sub * ATTN_SUB, (sub + 1) * ATTN_SUB)
        return (lax.dot_general(q0[rows], k, nt, preferred_element_type=F32),
                lax.dot_general(q1[rows], k, nt, preferred_element_type=F32))

    ahead = scores(0)
    for sub in range(nsub):
        rows = slice(sub * ATTN_SUB, (sub + 1) * ATTN_SUB)
        s0, s1 = ahead
        if sub + 1 < nsub:
            ahead = scores(sub + 1)
        e0 = jnp.exp2(s0 - jnp.max(s0, axis=-1, keepdims=True)).astype(BF16)
        e1 = jnp.exp2(s1 - jnp.max(s1, axis=-1, keepdims=True)).astype(BF16)
        ov0 = jnp.dot(e0, v, preferred_element_type=F32)
        ov1 = jnp.dot(e1, v, preferred_element_type=F32)
        r0 = 1.0 / ov0[:, V_DIM:V_DIM + 1]
        r1 = lam / ov1[:, V_DIM:V_DIM + 1]
        o = ov0[:, :V_DIM] * r0 - ov1[:, :V_DIM] * r1
        ms = jnp.mean(o * o, axis=-1, keepdims=True)
        o_ref[rows, :] = (o * lax.rsqrt(ms + EPS) * sg_ref[...] * (1.0 - lam_init)).astype(o_ref.dtype)

    prep(kn_ref, vn_ref, 1 - slot)


def _attention(proj, lamp, cos_t, sin_t, qg, kg, sg, b, s, lam_init):
    n = b * N_HEADS
    const = lambda i: (0, 0)
    cur = lambda col: (lambda i: (i // N_HEADS, col * N_HEADS + i % N_HEADS))
    nxt = lambda col: (lambda i: (jnp.minimum(i + 1, n - 1) // N_HEADS,
                                  col * N_HEADS + jnp.minimum(i + 1, n - 1) % N_HEADS))
    return pl.pallas_call(
        functools.partial(_attn_kernel, lam_init=lam_init),
        grid=(n,),
        in_specs=[pl.BlockSpec((4, HEAD_DIM), const),
                  pl.BlockSpec((s, LANES), cur(0)),
                  pl.BlockSpec((s, LANES), cur(1)),
                  pl.BlockSpec((s, LANES), cur(2)),
                  pl.BlockSpec((s, LANES), nxt(1)),
                  pl.BlockSpec((s, LANES), nxt(2)),
                  pl.BlockSpec((s, LANES), const),
                  pl.BlockSpec((s, LANES), const),
                  pl.BlockSpec((1, LANES), const),
                  pl.BlockSpec((1, LANES), const),
                  pl.BlockSpec((1, LANES), const)],
        out_specs=pl.BlockSpec((s, LANES), cur(0)),
        out_shape=jax.ShapeDtypeStruct((b * s, ATTN_WIDTH), BF16),
        scratch_shapes=[pltpu.VMEM((2, s, LANES), BF16), pltpu.VMEM((2, s, 2 * V_DIM), BF16)],
        compiler_params=_params(("arbitrary",)),
        name="diff_attn",
    )(lamp, proj, proj, proj, proj, proj, cos_t, sin_t, qg, kg, sg)


def _fourier_kernel(u_ref, cs_ref, cc_ref, g_ref, o_ref, pq, *, s):
    fg = FOURIER_GROUP

    @pl.when(pl.program_id(1) == 0)
    def _():
        for gi in range(N_FOURIER_GROUPS):
            r = jnp.dot(u_ref[:, gi * fg:(gi + 1) * fg], cc_ref[...], preferred_element_type=F32)
            pq[0:s, gi * fg:(gi + 1) * fg] = r[:, :fg].astype(BF16)
            pq[s:2 * s, gi * fg:(gi + 1) * fg] = r[:, fg:].astype(BF16)

    f = jnp.dot(cs_ref[...], pq[...], preferred_element_type=F32)
    for gi in range(N_FOURIER_GROUPS):
        fgi = f[:, gi * fg:(gi + 1) * fg]
        ms = jnp.mean(fgi * fgi, axis=-1, keepdims=True)
        o_ref[:, gi * fg:(gi + 1) * fg] = (
            fgi * lax.rsqrt(ms + EPS) * g_ref[:, gi * fg:(gi + 1) * fg]).astype(o_ref.dtype)


def _fourier(proj, cs, cc, g, b, s, tm):
    n = s // tm
    return pl.pallas_call(
        functools.partial(_fourier_kernel, s=s),
        grid=(b, n),
        in_specs=[pl.BlockSpec((s, FOURIER_WIDTH), lambda bi, i: (bi, 3)),
                  pl.BlockSpec((tm, 2 * s), lambda bi, i: (i, 0)),
                  pl.BlockSpec((FOURIER_GROUP, 2 * FOURIER_GROUP), lambda bi, i: (0, 0)),
                  pl.BlockSpec((1, FOURIER_WIDTH), lambda bi, i: (0, 0))],
        out_specs=pl.BlockSpec((tm, FOURIER_WIDTH), lambda bi, i: (bi * n + i, 0)),
        out_shape=jax.ShapeDtypeStruct((b * s, FOURIER_WIDTH), BF16),
        scratch_shapes=[pltpu.VMEM((2 * s, FOURIER_WIDTH), BF16)],
        compiler_params=_params(("parallel", "arbitrary")),
        name="fourier_mix",
    )(proj, cs, cc, g)


def _out_proj_kernel(x_ref, a_ref, f_ref, w1_ref, w2_ref, o_ref):
    o_ref[...] = (x_ref[...]
                  + jnp.dot(a_ref[...], w1_ref[...], preferred_element_type=F32)
                  + jnp.dot(f_ref[...], w2_ref[...], preferred_element_type=F32))


def _out_proj(x, a, f, w, layer, tm, tn):
    t, d = x.shape
    return pl.pallas_call(
        _out_proj_kernel,
        grid=(t // tm, d // tn),
        in_specs=[pl.BlockSpec((tm, tn), lambda i, j: (i, j)),
                  pl.BlockSpec((tm, ATTN_WIDTH), lambda i, j: (i, 0)),
                  pl.BlockSpec((tm, FOURIER_WIDTH), lambda i, j: (i, 0)),
                  pl.BlockSpec((None, ATTN_WIDTH, tn), lambda i, j: (layer, 0, j)),
                  pl.BlockSpec((None, FOURIER_WIDTH, tn), lambda i, j: (layer, 1, j))],
        out_specs=pl.BlockSpec((tm, tn), lambda i, j: (i, j)),
        out_shape=jax.ShapeDtypeStruct((t, d), F32),
        compiler_params=_params(("parallel", "arbitrary")),
        name="out_proj",
    )(x, a, f, w, w)


def _swiglu_partial(xn, wg_ref, wu_ref, wd_ref):
    gate = jnp.dot(xn, wg_ref[...], preferred_element_type=F32)
    up = jnp.dot(xn, wu_ref[...], preferred_element_type=F32)
    h = (gate * (1.0 / (1.0 + jnp.exp(-gate))) * up).astype(BF16)
    return jnp.dot(h, wd_ref[...], preferred_element_type=F32)


def _ffn_dense_kernel(x_ref, g_ref, wg_ref, wu_ref, wd_ref, o_ref, xn_ref):
    @pl.when(pl.program_id(1) == 0)
    def _():
        x = x_ref[...]
        ms = jnp.mean(x * x, axis=-1, keepdims=True)
        xn_ref[...] = (x * lax.rsqrt(ms + EPS) * g_ref[...]).astype(BF16)
        o_ref[...] = x

    o_ref[...] += _swiglu_partial(xn_ref[...], wg_ref, wu_ref, wd_ref)


def _ffn_dense(x, g, wg, wu, wd, layer, tm, tf):
    t, d = x.shape
    ff = wg.shape[2]
    return pl.pallas_call(
        _ffn_dense_kernel,
        grid=(t // tm, ff // tf),
        in_specs=[pl.BlockSpec((tm, d), lambda i, f: (i, 0)),
                  pl.BlockSpec((1, d), lambda i, f: (0, 0)),
                  pl.BlockSpec((None, d, tf), lambda i, f: (layer, 0, f)),
                  pl.BlockSpec((None, d, tf), lambda i, f: (layer, 0, f)),
                  pl.BlockSpec((None, tf, d), lambda i, f: (layer, f, 0))],
        out_specs=pl.BlockSpec((tm, d), lambda i, f: (i, 0)),
        out_shape=jax.ShapeDtypeStruct((t, d), F32),
        scratch_shapes=[pltpu.VMEM((tm, d), BF16)],
        compiler_params=_params(("parallel", "arbitrary")),
        name="swiglu_res",
    )(x, g, wg, wu, wd)


def _ffn_grouped_kernel(eid_ref, nv_ref, x_ref, wg_ref, wu_ref, wd_ref, *rest, n_cast):
    cast_in, o_ref, cast_out = rest[:n_cast], rest[n_cast], rest[n_cast + 1:]
    i = pl.program_id(0)
    f = pl.program_id(1)

    @pl.when(f == 0)
    def _():
        o_ref[...] = jnp.zeros_like(o_ref)

    @pl.when(i < nv_ref[0])
    def _():
        o_ref[...] += _swiglu_partial(x_ref[...], wg_ref, wu_ref, wd_ref)
        for src, dst in zip(cast_in, cast_out):
            dst[...] = src[...].astype(BF16)


def _cast_rows(rows, steps):
    for h in range(16, rows + 1, 16):
        if rows % h == 0 and rows // h <= steps:
            return h
    raise ValueError("no slab height fits")


def _ffn_grouped(xs, wg, wu, wd, layer, eid, nvalid, tm, tf, min_tiles, cast=(), cast_layer=0):
    r, d = xs.shape
    nf = wg.shape[3] // tf
    last = nf - 1
    cast_specs, cast_out_specs, cast_shapes = [], [], []
    for w in cast:
        _, rows, cols = w.shape
        h = _cast_rows(rows, min_tiles * nf)
        nblk = rows // h
        slab = lambda i, f, nblk=nblk: jnp.minimum(i * nf + f, nblk - 1)
        cast_specs.append(pl.BlockSpec((None, h, cols), lambda i, f, e, n, slab=slab: (cast_layer, slab(i, f), 0)))
        cast_out_specs.append(pl.BlockSpec((h, cols), lambda i, f, e, n, slab=slab: (slab(i, f), 0)))
        cast_shapes.append(jax.ShapeDtypeStruct((rows, cols), BF16))

    def tile(i, nv_ref):
        return jnp.minimum(i, nv_ref[0] - 1)

    def fsel(i, f, nv_ref):
        return jnp.where(i < nv_ref[0], f, last)

    def up_map(i, f, eid_ref, nv_ref):
        return (layer, eid_ref[tile(i, nv_ref)], 0, fsel(i, f, nv_ref))

    def down_map(i, f, eid_ref, nv_ref):
        return (layer, eid_ref[tile(i, nv_ref)], fsel(i, f, nv_ref), 0)

    grid_spec = pltpu.PrefetchScalarGridSpec(
        num_scalar_prefetch=2,
        grid=(r // tm, nf),
        in_specs=[pl.BlockSpec((tm, d), lambda i, f, e, n: (tile(i, n), 0)),
                  pl.BlockSpec((None, None, d, tf), up_map),
                  pl.BlockSpec((None, None, d, tf), up_map),
                  pl.BlockSpec((None, None, tf, d), down_map)] + cast_specs,
        out_specs=[pl.BlockSpec((tm, d), lambda i, f, e, n: (i, 0))] + cast_out_specs,
    )
    outs = pl.pallas_call(
        functools.partial(_ffn_grouped_kernel, n_cast=len(cast)),
        grid_spec=grid_spec,
        out_shape=[jax.ShapeDtypeStruct((r, d), F32)] + cast_shapes,
        compiler_params=_params(("arbitrary", "arbitrary")),
        name="swiglu_grouped",
    )(eid, nvalid, xs, wg, wu, wd, *cast)
    return outs[0], outs[1:]


def _router_kernel(x_ref, g_ref, wh_ref, wl_ref, ri_ref, rf_ref, cnt_ref, carry, *, tm):
    @pl.when(pl.program_id(0) == 0)
    def _():
        carry[...] = jnp.zeros_like(carry)

    x = x_ref[...]
    ms = jnp.mean(x * x, axis=-1, keepdims=True)
    xn = x * lax.rsqrt(ms + EPS) * g_ref[...]
    xh = xn.astype(BF16)
    xl = (xn - xh.astype(F32)).astype(BF16)
    wh = wh_ref[...]
    logits = (jnp.dot(xh, wh, preferred_element_type=F32)
              + jnp.dot(xl, wh, preferred_element_type=F32)
              + jnp.dot(xh, wl_ref[...], preferred_element_type=F32))

    lane = lax.broadcasted_iota(jnp.int32, (tm, LANES), 1).astype(F32)
    neg = jnp.float32(-jnp.inf)
    lg = jnp.where(lane < N_EXPERTS, logits, neg)
    m1 = jnp.max(lg, axis=-1, keepdims=True)
    i1 = jnp.min(jnp.where(lg == m1, lane, float(LANES)), axis=-1, keepdims=True)
    lg2 = jnp.where(lane == i1, neg, lg)
    m2 = jnp.max(lg2, axis=-1, keepdims=True)
    i2 = jnp.min(jnp.where(lg2 == m2, lane, float(LANES)), axis=-1, keepdims=True)
    e = jnp.exp(m2 - m1)
    g1 = 1.0 / (1.0 + e)
    g2 = e / (1.0 + e)

    sel1 = lane == i1
    sel2 = lane == i2
    onehot = jnp.where(sel1 | sel2, 1.0, 0.0)
    r_i = lax.broadcasted_iota(jnp.int32, (tm, tm), 0)
    c_i = lax.broadcasted_iota(jnp.int32, (tm, tm), 1)
    tri = jnp.where(r_i > c_i, 1.0, 0.0).astype(BF16)
    before = jnp.dot(tri, onehot.astype(BF16), preferred_element_type=F32) + carry[...]
    rank1 = jnp.sum(jnp.where(sel1, before, 0.0), axis=-1, keepdims=True)
    rank2 = jnp.sum(jnp.where(sel2, before, 0.0), axis=-1, keepdims=True)
    carry[...] = carry[...] + jnp.sum(onehot, axis=0, keepdims=True)
    cnt_ref[...] = carry[...]

    ri = jnp.where(lane == 0.0, i1, jnp.where(lane == 1.0, i2,
         jnp.where(lane == 2.0, rank1, jnp.where(lane == 3.0, rank2, 0.0))))
    ri_ref[...] = ri.astype(jnp.int32)
    rf_ref[...] = jnp.where(lane == 0.0, g1, jnp.where(lane == 1.0, g2, 0.0))


def _router(x, g, wh, wl, tm):
    t, d = x.shape
    return pl.pallas_call(
        functools.partial(_router_kernel, tm=tm),
        grid=(t // tm,),
        in_specs=[pl.BlockSpec((tm, d), lambda i: (i, 0)),
                  pl.BlockSpec((1, d), lambda i: (0, 0)),
                  pl.BlockSpec((d, LANES), lambda i: (0, 0)),
                  pl.BlockSpec((d, LANES), lambda i: (0, 0))],
        out_specs=[pl.BlockSpec((tm, LANES), lambda i: (i, 0)),
                   pl.BlockSpec((tm, LANES), lambda i: (i, 0)),
                   pl.BlockSpec((1, LANES), lambda i: (0, 0))],
        out_shape=[jax.ShapeDtypeStruct((t, LANES), jnp.int32),
                   jax.ShapeDtypeStruct((t, LANES), F32),
                   jax.ShapeDtypeStruct((1, LANES), F32)],
        scratch_shapes=[pltpu.VMEM((1, LANES), F32)],
        compiler_params=_params(("arbitrary",)),
        name="router",
    )(x, g, wh, wl)


def _dispatch_kernel(src_ref, nxt_ref, g_ref, x_hbm, o_ref, buf, sem, *, chunk):
    i = pl.program_id(0)
    slot = i % 2

    def gather(idx_ref, s):
        def issue(j, c):
            pltpu.make_async_copy(x_hbm.at[pl.ds(idx_ref[j], 1)], buf.at[s, pl.ds(j, 1)], sem.at[s]).start()
            return c

        lax.fori_loop(0, chunk, issue, 0, unroll=8)

    @pl.when(i == 0)
    def _():
        gather(src_ref, slot)

    @pl.when(i + 1 < pl.num_programs(0))
    def _():
        gather(nxt_ref, 1 - slot)

    pltpu.make_async_copy(x_hbm.at[pl.ds(0, chunk)], buf.at[slot], sem.at[slot]).wait()
    x = buf[slot]
    ms = jnp.mean(x * x, axis=-1, keepdims=True)
    o_ref[...] = (x * lax.rsqrt(ms + EPS) * g_ref[...]).astype(o_ref.dtype)


def _dispatch(x, g, src, chunk):
    r = src.shape[0]
    d = x.shape[1]
    n = r // chunk
    return pl.pallas_call(
        functools.partial(_dispatch_kernel, chunk=chunk),
        grid=(n,),
        in_specs=[pl.BlockSpec((chunk,), lambda i: (i,), memory_space=pltpu.SMEM),
                  pl.BlockSpec((chunk,), lambda i: (jnp.minimum(i + 1, n - 1),), memory_space=pltpu.SMEM),
                  pl.BlockSpec((1, d), lambda i: (0, 0)),
                  pl.BlockSpec(memory_space=pl.ANY)],
        out_specs=pl.BlockSpec((chunk, d), lambda i: (i, 0)),
        out_shape=jax.ShapeDtypeStruct((r, d), BF16),
        scratch_shapes=[pltpu.VMEM((2, chunk, d), F32), pltpu.SemaphoreType.DMA((2,))],
        compiler_params=_params(("arbitrary",)),
        name="moe_dispatch",
    )(src, src, g, x)


def _combine_kernel(pos_ref, nxt_ref, x_ref, rf_ref, y_hbm, *rest, tc, split_tile):
    outs, (buf, sem) = rest[:-2], rest[-2:]
    i = pl.program_id(0)
    slot = i % 2

    def gather(idx_ref, s):
        def issue(t, c):
            for k in range(TOP_K):
                pltpu.make_async_copy(y_hbm.at[pl.ds(idx_ref[TOP_K * t + k], 1)],
                                      buf.at[s, k, pl.ds(t, 1)], sem.at[s]).start()
            return c

        lax.fori_loop(0, tc, issue, 0, unroll=4)

    @pl.when(i == 0)
    def _():
        gather(pos_ref, slot)

    @pl.when(i + 1 < pl.num_programs(0))
    def _():
        gather(nxt_ref, 1 - slot)

    for k in range(TOP_K):
        pltpu.make_async_copy(y_hbm.at[pl.ds(0, tc)], buf.at[slot, k], sem.at[slot]).wait()

    rf = rf_ref[...]
    res = x_ref[...] + rf[:, 0:1] * buf[slot, 0] + rf[:, 1:2] * buf[slot, 1]
    if split_tile is None:
        outs[0][...] = res
    else:
        @pl.when(pl.program_id(0) < split_tile)
        def _():
            outs[0][...] = res

        @pl.when(pl.program_id(0) >= split_tile)
        def _():
            outs[1][...] = res


def _combine(x, rf, y, pos_flat, tc, split_rows=None):
    t, d = x.shape
    if split_rows is None:
        split_tile = None
        out_specs = pl.BlockSpec((tc, d), lambda i: (i, 0))
        out_shape = jax.ShapeDtypeStruct((t, d), F32)
    else:
        split_tile = split_rows // tc
        out_specs = [pl.BlockSpec((tc, d), lambda i: (jnp.minimum(i, split_tile - 1), 0)),
                     pl.BlockSpec((tc, d), lambda i: (jnp.maximum(i - split_tile, 0), 0))]
        out_shape = [jax.ShapeDtypeStruct((split_rows, d), F32), jax.ShapeDtypeStruct((t - split_rows, d), F32)]
    n = t // tc
    return pl.pallas_call(
        functools.partial(_combine_kernel, tc=tc, split_tile=split_tile),
        grid=(n,),
        in_specs=[pl.BlockSpec((TOP_K * tc,), lambda i: (i,), memory_space=pltpu.SMEM),
                  pl.BlockSpec((TOP_K * tc,), lambda i: (jnp.minimum(i + 1, n - 1),), memory_space=pltpu.SMEM),
                  pl.BlockSpec((tc, d), lambda i: (i, 0)),
                  pl.BlockSpec((tc, LANES), lambda i: (i, 0)),
                  pl.BlockSpec(memory_space=pl.ANY)],
        out_specs=out_specs,
        out_shape=out_shape,
        scratch_shapes=[pltpu.VMEM((2, TOP_K, tc, d), F32), pltpu.SemaphoreType.DMA((2,))],
        compiler_params=_params(("arbitrary",)),
        name="moe_combine",
    )(pos_flat, pos_flat, x, rf, y)


def _rope_tables(s):
    pos = jnp.arange(s, dtype=F32)
    inv = ROPE_THETA ** (-jnp.arange(0, HEAD_DIM, 2, dtype=F32) / HEAD_DIM)
    ang = pos[:, None] * inv[None, :]
    cos, sin = jnp.cos(ang), jnp.sin(ang)
    cos_t = jnp.tile(cos, (1, LANES // (HEAD_DIM // 2)))
    sin_t = jnp.tile(jnp.concatenate([-sin, sin], axis=1), (1, LANES // HEAD_DIM))
    return cos_t, sin_t


def _dft_tables(n):
    k = jnp.arange(n, dtype=jnp.int32)
    ang = ((k[:, None] * k[None, :]) % n).astype(F32) * (2.0 * math.pi / n)
    scale = 1.0 / math.sqrt(n)
    return jnp.cos(ang) * scale, jnp.sin(ang) * scale


def _tiles(t, s):
    return dict(in_tm=min(1024, t), in_tn=1024, four_tm=min(512, s),
                out_tm=min(1024, t), out_tn=1024, ffn_tm=min(1024, t), ffn_tf=512,
                moe_tm=min(512, t), moe_tf=1024, router_tm=min(512, t),
                gather_chunk=min(512, t), combine_tc=min(256, t))


def _moe_layer(x, g, w_router, wg, wu, wd, layer, cfg, split_rows=None, cast=(), cast_layer=0):
    t, d = x.shape
    tm = cfg["moe_tm"]
    wr = jnp.pad(w_router, ((0, 0), (0, LANES - N_EXPERTS)))
    wr_hi = wr.astype(BF16)
    wr_lo = (wr - wr_hi.astype(F32)).astype(BF16)
    ri, rf, cnt = _router(x, g, wr_hi, wr_lo, cfg["router_tm"])

    n_tiles = (TOP_K * t) // tm + N_EXPERTS
    counts = cnt[0, :N_EXPERTS].astype(jnp.int32)
    padded = ((counts + tm - 1) // tm) * tm
    ends = jnp.cumsum(padded)
    offs = ends - padded
    pos = offs[ri[:, 0:TOP_K]] + ri[:, TOP_K:2 * TOP_K]
    pos_flat = pos.reshape(-1)
    tok = jnp.repeat(jnp.arange(t, dtype=jnp.int32), TOP_K)
    src = jnp.zeros((n_tiles * tm,), jnp.int32).at[pos_flat].set(tok)
    tile_start = jnp.arange(n_tiles, dtype=jnp.int32) * tm
    eid = jnp.minimum(jnp.sum(tile_start[:, None] >= ends[None, :], axis=1), N_EXPERTS - 1).astype(jnp.int32)
    nvalid = (ends[-1:] // tm).astype(jnp.int32)

    xs = _dispatch(x, g, src, cfg["gather_chunk"])
    ys, cast_out = _ffn_grouped(xs, wg, wu, wd, layer, eid, nvalid, tm, cfg["moe_tf"],
                                (TOP_K * t) // tm, cast, cast_layer)
    return _combine(x, rf, ys, pos_flat, cfg["combine_tc"], split_rows), cast_out


def _trunk(x3, split_batch, attn_norm_g, w_in, q_norm_g, k_norm_g, lambda_q1, lambda_k1, lambda_q2, lambda_k2,
           subln_g, fourier_norm_g, w_out, ffn_norm_g, dense_w_gate, dense_w_up, dense_w_down,
           router_w, moe_w_gate, moe_w_up, moe_w_down):
    b, s, d = x3.shape
    t = b * s
    depth = w_in.shape[0]
    cfg = _tiles(t, s)
    x = x3.reshape(t, d)

    cos_t, sin_t = _rope_tables(s)
    cs_c, cs_s = _dft_tables(s)
    cs = jnp.concatenate([cs_c, -cs_s], axis=1).astype(BF16)
    cc_c, cc_s = _dft_tables(FOURIER_GROUP)
    cc = jnp.concatenate([cc_c, cc_s], axis=1).astype(BF16)

    rep = LANES // HEAD_DIM
    split_rows = split_batch * s
    w_in, w_out = w_in.astype(BF16), w_out.astype(BF16)
    dense_w = [w.astype(BF16) for w in (dense_w_gate, dense_w_up, dense_w_down)]
    moe_f32 = (moe_w_gate, moe_w_up, moe_w_down)
    moe_shapes = [w.shape[1:] for w in moe_f32]
    moe_views = [w.reshape(w.shape[0], w.shape[1] * w.shape[2], w.shape[3]) for w in moe_f32]
    moe_w = [w[0:1].astype(BF16) for w in moe_f32]
    n_moe = moe_w_gate.shape[0]

    for l in range(depth):
        lam_init = 0.8 - 0.6 * math.exp(-0.3 * l)
        proj = _in_proj(x, attn_norm_g[l][None, :], w_in, l, cfg["in_tm"], cfg["in_tn"])
        lamp = jnp.stack([lambda_q1[l], lambda_k1[l], lambda_q2[l], lambda_k2[l]])
        a = _attention(proj, lamp, cos_t, sin_t, jnp.tile(q_norm_g[l], rep)[None, :],
                       jnp.tile(k_norm_g[l], rep)[None, :], subln_g[l][None, :], b, s, lam_init)
        f = _fourier(proj, cs, cc, fourier_norm_g[l][None, :], b, s, cfg["four_tm"])
        x = _out_proj(x, a, f, w_out, l, cfg["out_tm"], cfg["out_tn"])
        gf = ffn_norm_g[l][None, :]
        i = l // 2
        if l % 2 == 0:
            x = _ffn_dense(x, gf, *dense_w, i, cfg["ffn_tm"], cfg["ffn_tf"])
        else:
            last = l == depth - 1
            more = i + 1 < n_moe
            x, cast = _moe_layer(x, gf, router_w[i], *moe_w, 0, cfg, split_rows if last else None,
                                 moe_views if more else (), i + 1)
            moe_w = [w.reshape((1,) + shp) for w, shp in zip(cast, moe_shapes)]
    if isinstance(x, (list, tuple)):
        y0, y1 = x
    else:
        y0, y1 = x[:split_rows], x[split_rows:]
    return y0.reshape(split_batch, s, d), y1.reshape(b - split_batch, s, d)


def kernel(x_prompt, x_sample, attn_norm_g, w_in, q_norm_g, k_norm_g, lambda_q1, lambda_k1, lambda_q2, lambda_k2, subln_g, fourier_norm_g, w_out, ffn_norm_g, dense_w_gate, dense_w_up, dense_w_down, router_w, moe_w_gate, moe_w_up, moe_w_down):
    nb = x_prompt.shape[0]
    x = jnp.concatenate([x_prompt, x_sample], axis=0)
    return _trunk(x, nb, attn_norm_g, w_in, q_norm_g, k_norm_g, lambda_q1, lambda_k1, lambda_q2, lambda_k2,
                  subln_g, fourier_norm_g, w_out, ffn_norm_g, dense_w_gate, dense_w_up, dense_w_down,
                  router_w, moe_w_gate, moe_w_up, moe_w_down)
```
